```python
import jax
import jax.numpy as jnp
from jax import lax
import numpy as np

D_MODEL = 2048
BATCH = 4
SEQ = 2048
DEPTH = 2
DEC_BATCH = 128
DEC_SEQ = 4
PAST_LEN = 2048
PAGE_SIZE = 128

D_MIX = D_MODEL
D_ATT = D_MIX // 2
HEAD_DIM = 128
N_HEADS = D_ATT // HEAD_DIM
D_POOL = D_MIX // 4
POOL_WINDOWS = (2, 4, 8, 16)
N_POOL_GROUPS = len(POOL_WINDOWS)
POOL_GROUP = D_POOL // N_POOL_GROUPS
POOL_BUF = max(POOL_WINDOWS) - 1
D_RNN = D_MIX - D_ATT - D_POOL
N_RNN_BLOCKS = 4
RNN_BLOCK = D_RNN // N_RNN_BLOCKS
CONV_WIDTH = 4
RG_C = 8.0
D_FF = 4 * D_MODEL
D_PLE = 256
Q_BLOCK = 128
SB_BIAS_INIT = -6.0
EPS = 1e-6
IN_SPLITS = [D_ATT, 2 * D_ATT, 3 * D_ATT, 3 * D_ATT + D_POOL, 3 * D_ATT + D_POOL + D_RNN]
D_IN = 3 * D_ATT + D_POOL + 2 * D_RNN

kernel_name = "hybrid_sb_pool_rglru_decoder_step"


def pool_pages(n_pages):
    return (DEC_BATCH * n_pages * 5) // 4


def _normalize(x):
    xf = x.astype(jnp.float32)
    return xf * lax.rsqrt(jnp.mean(xf * xf, axis=-1, keepdims=True) + EPS)


def rms_norm(x, g):
    return (_normalize(x) * g.astype(jnp.float32)).astype(x.dtype)


def group_rms_norm(y, g):
    parts = jnp.split(y, [D_ATT, D_ATT + D_POOL], axis=-1)
    normed = jnp.concatenate([_normalize(pt) for pt in parts], axis=-1)
    return (normed * g.astype(jnp.float32)).astype(y.dtype)


def stick_breaking_attention(q, k, v, b_sb, q_pos0):
    b, tq, h, hd = q.shape
    tk = k.shape[1]
    blk = min(Q_BLOCK, tq)
    nb = tq // blk
    k_pos = jnp.arange(tk)
    scale = hd ** -0.5
    bias = b_sb.astype(jnp.float32)[None, :, None, None]

    def one_block(args):
        qb, qpos = args
        z = jnp.einsum('bqhd,bkhd->bhqk', qb, k, preferred_element_type=jnp.float32) * scale + bias
        mask = k_pos[None, :] < qpos[:, None]
        log_stay = jnp.where(mask, jax.nn.log_sigmoid(-z), 0.0)
        later = lax.cumsum(log_stay, axis=3, reverse=True) - log_stay
        w = jnp.where(mask, jnp.exp(jax.nn.log_sigmoid(z) + later), 0.0)
        return jnp.einsum('bhqk,bkhd->bqhd', w.astype(v.dtype), v)

    qb = q.reshape(b, nb, blk, h, hd).transpose(1, 0, 2, 3, 4)
    qpos = (q_pos0 + jnp.arange(tq)).reshape(nb, blk)
    out = lax.map(one_block, (qb, qpos))
    return out.transpose(1, 0, 2, 3, 4).reshape(b, tq, h, hd)


def multiscale_pool(u, buf, pos0, w_pool, s_pool):
    b, t, _ = u.shape
    ext = jnp.concatenate([buf, u], axis=1)
    csum = jnp.cumsum(ext.astype(jnp.float32), axis=1)
    csum = jnp.pad(csum, ((0, 0), (1, 0), (0, 0)))
    end = csum[:, POOL_BUF + 1:]
    pos = pos0 + jnp.arange(t)
    means = []
    for gi, win in enumerate(POOL_WINDOWS):
        lo, hi = gi * POOL_GROUP, (gi + 1) * POOL_GROUP
        start = csum[:, POOL_BUF + 1 - win:POOL_BUF + 1 - win + t, lo:hi]
        cnt = jnp.minimum(win, pos + 1).astype(jnp.float32)[None, :, None]
        means.append((end[..., lo:hi] - start) / cnt)
    pooled = (jnp.concatenate(means, axis=-1) - u.astype(jnp.float32)).astype(u.dtype)
    pooled = pooled.reshape(b, t, N_POOL_GROUPS, POOL_GROUP)
    y = jnp.einsum('btgc,gcd->btgd', pooled, w_pool).reshape(b, t, D_POOL)
    return y * s_pool, ext[:, -POOL_BUF:]


def rglru_branch(u, gate_in, conv_buf, h_prev, conv_w, conv_b, w_a, b_a, w_x, b_x, lam):
    b, t, _ = u.shape
    ext = jnp.concatenate([conv_buf, u], axis=1)
    xc = conv_b + ext[:, 0:t] * conv_w[0]
    for j in range(1, CONV_WIDTH):
        xc = xc + ext[:, j:j + t] * conv_w[j]
    xb = xc.reshape(b, t, N_RNN_BLOCKS, RNN_BLOCK)
    r = jax.nn.sigmoid(jnp.einsum('btnc,ncd->btnd', xb, w_a).reshape(b, t, D_RNN) + b_a)
    i = jax.nn.sigmoid(jnp.einsum('btnc,ncd->btnd', xb, w_x).reshape(b, t, D_RNN) + b_x)
    log_a = -RG_C * r.astype(jnp.float32) * jax.nn.softplus(-lam.astype(jnp.float32))
    a = jnp.exp(log_a)
    inp = jnp.sqrt(-jnp.expm1(2.0 * log_a)) * (i * xc).astype(jnp.float32)

    def step(h, ab):
        a_t, b_t = ab
        h = a_t * h + b_t
        return h, h

    h_last, hs = lax.scan(step, h_prev.astype(jnp.float32),
                          (a.transpose(1, 0, 2), inp.transpose(1, 0, 2)))
    y = hs.transpose(1, 0, 2).astype(u.dtype) * jax.nn.gelu(gate_in)
    return y, ext[:, -(CONV_WIDTH - 1):], h_last.astype(h_prev.dtype)


def decoder_layer(x, p, pos0, k_past, v_past, pool_buf, conv_buf, h_prev,
                  g_pre_mix, w_in, b_sb, w_pool, s_pool, conv_w, conv_b, w_a, b_a, w_x, b_x, lam,
                  g_grp, w_out, g_post_mix, g_pre_ffn, w_up, w_down, g_post_ffn,
                  w_ple, g_ple, w_ple_gate):
    b, t, _ = x.shape
    hn = rms_norm(x, g_pre_mix)
    proj = hn @ w_in
    q, k, v, u_pool, u_rnn, g_rnn = jnp.split(proj, IN_SPLITS, axis=-1)
    q = q.reshape(b, t, N_HEADS, HEAD_DIM)
    k = k.reshape(b, t, N_HEADS, HEAD_DIM)
    v = v.reshape(b, t, N_HEADS, HEAD_DIM)
    k_all = jnp.concatenate([k_past, k], axis=1)
    v_all = jnp.concatenate([v_past, v], axis=1)
    o_att = stick_breaking_attention(q, k_all, v_all, b_sb, pos0).reshape(b, t, D_ATT)
    o_pool, new_pool = multiscale_pool(u_pool, pool_buf, pos0, w_pool, s_pool)
    o_rnn, new_conv, h_last = rglru_branch(u_rnn, g_rnn, conv_buf, h_prev, conv_w, conv_b,
                                           w_a, b_a, w_x, b_x, lam)
    mixed = group_rms_norm(jnp.concatenate([o_att, o_pool, o_rnn], axis=-1), g_grp)
    x = x + rms_norm(mixed @ w_out, g_post_mix)
    f = jnp.square(jax.nn.relu(rms_norm(x, g_pre_ffn) @ w_up)) @ w_down
    x = x + rms_norm(f, g_post_ffn)
    e = rms_norm(p @ w_ple, g_ple)
    x = x + e * jax.nn.sigmoid(x @ w_ple_gate)
    return x, (k, v, new_pool, new_conv, h_last)


def setup_inputs(seed: int = 0) -> dict:
    key = jax.random.key(seed)
    ks = list(jax.random.split(key, 40))
    f32 = jnp.float32

    def nrm(shape, scale=1.0):
        return jax.random.normal(ks.pop(), shape, f32) * scale

    def gain(shape):
        return 1.0 + nrm(shape, 0.02)

    n_pages = PAST_LEN // PAGE_SIZE
    n_pool = pool_pages(n_pages)
    lam_u = jax.random.uniform(ks.pop(), (DEPTH, D_RNN), f32, 0.9, 0.999)
    lam_s = lam_u ** (1.0 / RG_C)
    page_table = jax.random.permutation(ks.pop(), n_pool)[:DEC_BATCH * n_pages]
    page_table = page_table.reshape(DEC_BATCH, n_pages).astype(jnp.int32)
    return {
        "x_prompt": nrm((BATCH, SEQ, D_MODEL)),
        "x_sample": nrm((DEC_BATCH, DEC_SEQ, D_MODEL)),
        "p_prompt": nrm((DEPTH, BATCH, SEQ, D_PLE)),
        "p_sample": nrm((DEPTH, DEC_BATCH, DEC_SEQ, D_PLE)),
        "cache_k": nrm((DEPTH, n_pool, PAGE_SIZE, N_HEADS, HEAD_DIM)),
        "cache_v": nrm((DEPTH, n_pool, PAGE_SIZE, N_HEADS, HEAD_DIM)),
        "page_table": page_table,
        "state_pool": nrm((DEPTH, DEC_BATCH, POOL_BUF, D_POOL)),
        "state_conv": nrm((DEPTH, DEC_BATCH, CONV_WIDTH - 1, D_RNN)),
        "state_h": nrm((DEPTH, DEC_BATCH, D_RNN), 0.5),
        "g_pre_mix": gain((DEPTH, D_MODEL)),
        "w_in": nrm((DEPTH, D_MODEL, D_IN), D_MODEL ** -0.5),
        "b_sb": SB_BIAS_INIT + nrm((DEPTH, N_HEADS), 0.1),
        "w_pool": nrm((DEPTH, N_POOL_GROUPS, POOL_GROUP, POOL_GROUP), POOL_GROUP ** -0.5),
        "s_pool": 1.0 + nrm((DEPTH, D_POOL), 0.1),
        "conv_w": nrm((DEPTH, CONV_WIDTH, D_RNN), CONV_WIDTH ** -0.5),
        "conv_b": nrm((DEPTH, D_RNN), 0.01),
        "w_a": nrm((DEPTH, N_RNN_BLOCKS, RNN_BLOCK, RNN_BLOCK), RNN_BLOCK ** -0.5),
        "b_a": nrm((DEPTH, D_RNN), 0.01),
        "w_x": nrm((DEPTH, N_RNN_BLOCKS, RNN_BLOCK, RNN_BLOCK), RNN_BLOCK ** -0.5),
        "b_x": nrm((DEPTH, D_RNN), 0.01),
        "lam": jnp.log(lam_s) - jnp.log1p(-lam_s),
        "g_grp": gain((DEPTH, D_MIX)),
        "w_out": nrm((DEPTH, D_MIX, D_MODEL), D_MIX ** -0.5),
        "g_post_mix": gain((DEPTH, D_MODEL)),
        "g_pre_ffn": gain((DEPTH, D_MODEL)),
        "w_up": nrm((DEPTH, D_MODEL, D_FF), D_MODEL ** -0.5),
        "w_down": nrm((DEPTH, D_FF, D_MODEL), D_FF ** -0.5),
        "g_post_ffn": gain((DEPTH, D_MODEL)),
        "w_ple": nrm((DEPTH, D_PLE, D_MODEL), D_PLE ** -0.5),
        "g_ple": gain((DEPTH, D_MODEL)),
        "w_ple_gate": nrm((DEPTH, D_MODEL, D_MODEL), D_MODEL ** -0.5),
    }


def reference(x_prompt, x_sample, p_prompt, p_sample, cache_k, cache_v, page_table,
              state_pool, state_conv, state_h, g_pre_mix, w_in, b_sb, w_pool, s_pool, conv_w, conv_b,
              w_a, b_a, w_x, b_x, lam, g_grp, w_out, g_post_mix, g_pre_ffn, w_up, w_down,
              g_post_ffn, w_ple, g_ple, w_ple_gate):
    dec_b, n_pages = page_table.shape
    past_len = n_pages * PAGE_SIZE
    b_p = x_prompt.shape[0]
    dt = x_prompt.dtype
    empty_kv = jnp.zeros((b_p, 0, N_HEADS, HEAD_DIM), dt)
    pool0 = jnp.zeros((b_p, POOL_BUF, D_POOL), dt)
    conv0 = jnp.zeros((b_p, CONV_WIDTH - 1, D_RNN), dt)
    h0 = jnp.zeros((b_p, D_RNN), dt)
    yp, ys = x_prompt, x_sample
    new_p = [[] for _ in range(5)]
    new_s = [[] for _ in range(5)]
    for l in range(DEPTH):
        weights = (g_pre_mix[l], w_in[l], b_sb[l], w_pool[l], s_pool[l], conv_w[l], conv_b[l],
                   w_a[l], b_a[l], w_x[l], b_x[l], lam[l], g_grp[l], w_out[l],
                   g_post_mix[l], g_pre_ffn[l], w_up[l], w_down[l], g_post_ffn[l],
                   w_ple[l], g_ple[l], w_ple_gate[l])
        yp, st_p = decoder_layer(yp, p_prompt[l], 0, empty_kv, empty_kv, pool0, conv0, h0, *weights)
        k_past = cache_k[l][page_table].reshape(dec_b, past_len, N_HEADS, HEAD_DIM)
        v_past = cache_v[l][page_table].reshape(dec_b, past_len, N_HEADS, HEAD_DIM)
        ys, st_s = decoder_layer(ys, p_sample[l], past_len, k_past, v_past,
                                 state_pool[l], state_conv[l], state_h[l], *weights)
        for j in range(5):
            new_p[j].append(st_p[j])
            new_s[j].append(st_s[j])
    return (yp, ys,
            jnp.stack(new_p[0]), jnp.stack(new_p[1]), jnp.stack(new_p[2]),
            jnp.stack(new_p[3]), jnp.stack(new_p[4]),
            jnp.stack(new_s[0]), jnp.stack(new_s[1]), jnp.stack(new_s[2]),
            jnp.stack(new_s[3]), jnp.stack(new_s[4]))
```

```python
import functools

import jax
import jax.numpy as jnp
from jax import lax
from jax.experimental import pallas as pl
from jax.experimental.pallas import tpu as pltpu

F32 = jnp.float32
BF16 = jnp.bfloat16

D_MODEL = 2048
D_ATT = 1024
HEAD_DIM = 128
N_HEADS = 8
D_POOL = 512
POOL_WINDOWS = (2, 4, 8, 16)
POOL_GROUP = 128
POOL_BUF = 15
D_RNN = 512
RNN_BLOCK = 128
CONV_WIDTH = 4
RG_C = 8.0
D_FF = 8192
D_PLE = 256
PAGE_SIZE = 128
EPS = 1e-6
D_REST = D_POOL + 2 * D_RNN
D_IN = 3 * D_ATT + D_REST
SCALE = HEAD_DIM ** -0.5

VMEM_LIMIT_V7X = 56 * 1024 * 1024
ROW_TILE = 512
FF_TILE = 1024
ATT_TILE = 256
MIX_TILE = 256
HIST_ROWS = 16
CONV_HIST = 8


def _params(sem, vmem=VMEM_LIMIT_V7X):
    return pltpu.CompilerParams(dimension_semantics=sem, vmem_limit_bytes=vmem)


def _resident(shape):
    return pl.BlockSpec(shape, lambda *_: (0,) * len(shape), pipeline_mode=pl.Buffered(1))


def _rms(x, g):
    return x * lax.rsqrt(jnp.mean(x * x, axis=-1, keepdims=True) + EPS) * g


def _normalize(x):
    return x * lax.rsqrt(jnp.mean(x * x, axis=-1, keepdims=True) + EPS)


def _dot(a, b):
    return jnp.dot(a, b, preferred_element_type=F32)


def _dot_nt(a, b):
    return lax.dot_general(a, b, (((1,), (1,)), ((), ())), preferred_element_type=F32)


def _softplus(x):
    return jnp.maximum(x, 0.0) + jnp.log1p(jnp.exp(-jnp.abs(x)))


def _gelu_tanh(x):
    c = 0.7978845608028654
    return 0.5 * x * (1.0 + jnp.tanh(c * (x + 0.044715 * (x * x * x))))


def _in_proj_kernel(x_ref, g_ref, w_ref, q_ref, k_ref, v_ref, kh_ref, vh_ref, r_ref):
    xn = _rms(x_ref[...], g_ref[...]).astype(BF16)
    q_ref[...] = _dot(xn, w_ref[:, 0:D_ATT]).astype(BF16)
    k = _dot(xn, w_ref[:, D_ATT:2 * D_ATT])
    v = _dot(xn, w_ref[:, 2 * D_ATT:3 * D_ATT])
    k_ref[...] = k.astype(BF16)
    v_ref[...] = v.astype(BF16)
    for h in range(N_HEADS):
        kh_ref[:, h, :] = k[:, h * HEAD_DIM:(h + 1) * HEAD_DIM]
        vh_ref[:, h, :] = v[:, h * HEAD_DIM:(h + 1) * HEAD_DIM]
    r_ref[...] = _dot(xn, w_ref[:, 3 * D_ATT:D_IN])


def _in_proj(x, g, w_bf):
    n = x.shape[0]
    row = lambda width: pl.BlockSpec((ROW_TILE, width), lambda i: (i, 0))
    head_rows = pl.BlockSpec((ROW_TILE, N_HEADS, HEAD_DIM), lambda i: (i, 0, 0))
    return pl.pallas_call(
        _in_proj_kernel,
        grid=(n // ROW_TILE,),
        in_specs=[row(D_MODEL), _resident((1, D_MODEL)), _resident((D_MODEL, D_IN))],
        out_specs=[row(D_ATT), row(D_ATT), row(D_ATT), head_rows, head_rows, row(D_REST)],
        out_shape=[jax.ShapeDtypeStruct((n, D_ATT), BF16)] * 3
        + [jax.ShapeDtypeStruct((n, N_HEADS, HEAD_DIM), F32)] * 2
        + [jax.ShapeDtypeStruct((n, D_REST), F32)],
        compiler_params=_params(("arbitrary",)),
        name="in_proj",
    )(x, g, w_bf)


def _strict_upper_ones(n):
    j = lax.broadcasted_iota(jnp.int32, (n, n), 0)
    s = lax.broadcasted_iota(jnp.int32, (n, n), 1)
    return jnp.where(j > s, 1.0, 0.0).astype(BF16)


def _sb_tile(q_bf, k_bf, v_bf, bias, carry, tri, mask):
    z = _dot_nt(q_bf, k_bf) * SCALE + bias
    l1p = jnp.log1p(jnp.exp(-jnp.abs(z)))
    log_beta = jnp.minimum(z, 0.0) - l1p
    log_stay = -jnp.maximum(z, 0.0) - l1p
    if mask is not None:
        log_stay = jnp.where(mask, log_stay, 0.0)
    hi = log_stay.astype(BF16)
    lo = (log_stay - hi.astype(F32)).astype(BF16)
    later = _dot(hi, tri) + _dot(lo, tri) + carry
    w = jnp.exp(log_beta + later)
    if mask is not None:
        w = jnp.where(mask, w, 0.0)
    out = _dot(w.astype(BF16), v_bf)
    return out, carry + jnp.sum(log_stay, axis=-1, keepdims=True)


def _attn_prompt_kernel(bias_ref, q_ref, k_ref, v_ref, o_ref):
    t = ATT_TILE
    n_blk = q_ref.shape[0] // t
    bias = bias_ref[pl.program_id(1)]
    tri = _strict_upper_ones(t)
    row = lax.broadcasted_iota(jnp.int32, (t, t), 0)
    col = lax.broadcasted_iota(jnp.int32, (t, t), 1)
    diag_mask = col < row

    def q_block(qi, _):
        q0 = pl.multiple_of(qi * t, t)
        q_bf = q_ref[pl.ds(q0, t), :]
        acc, carry = _sb_tile(q_bf, k_ref[pl.ds(q0, t), :], v_ref[pl.ds(q0, t), :],
                              bias, jnp.zeros((t, 1), F32), tri, diag_mask)

        def k_block(it, state):
            acc, carry = state
            k0 = pl.multiple_of((qi - 1 - it) * t, t)
            out, carry = _sb_tile(q_bf, k_ref[pl.ds(k0, t), :], v_ref[pl.ds(k0, t), :],
                                  bias, carry, tri, None)
            return acc + out, carry

        acc, _ = lax.fori_loop(0, qi, k_block, (acc, carry))
        o_ref[pl.ds(q0, t), :] = acc
        return 0

    lax.fori_loop(0, n_blk, q_block, 0)


def _attn_prompt(q, k, v, b_sb, batch, seq):
    blk = pl.BlockSpec((seq, HEAD_DIM), lambda b, h: (b, h))
    return pl.pallas_call(
        _attn_prompt_kernel,
        grid=(batch, N_HEADS),
        in_specs=[pl.BlockSpec(memory_space=pltpu.SMEM), blk, blk, blk],
        out_specs=blk,
        out_shape=jax.ShapeDtypeStruct((batch * seq, D_ATT), F32),
        compiler_params=_params(("arbitrary", "arbitrary")),
        name="attn_prompt",
    )(b_sb, q, k, v)


def _attn_sample_kernel(n_pages, pt_ref, bias_ref, q_ref, kn_ref, vn_ref, *rest):
    del pt_ref
    k_refs = rest[:n_pages]
    v_refs = rest[n_pages:2 * n_pages]
    o_ref, kpad_ref, vpad_ref = rest[2 * n_pages:]
    b = pl.program_id(0)
    rows = 8 * N_HEADS
    t = PAGE_SIZE

    @pl.when(b == 0)
    def _():
        kpad_ref[...] = jnp.zeros_like(kpad_ref)
        vpad_ref[...] = jnp.zeros_like(vpad_ref)

    def rows_by_heads(ref):
        return jnp.concatenate([ref[:, h, :] for h in range(N_HEADS)], axis=1)

    kpad_ref[0:8, :] = rows_by_heads(kn_ref)
    vpad_ref[0:8, :] = rows_by_heads(vn_ref)

    q8 = q_ref[0]
    q_rep = jnp.concatenate([q8] * N_HEADS, axis=0)
    r_id = lax.broadcasted_iota(jnp.int32, (rows, D_ATT), 0)
    c_id = lax.broadcasted_iota(jnp.int32, (rows, D_ATT), 1)
    q_bd = jnp.where(c_id // HEAD_DIM == r_id // 8, q_rep, 0.0).astype(BF16)
    bias = bias_ref[...]
    tri = _strict_upper_ones(t)

    row = lax.broadcasted_iota(jnp.int32, (rows, t), 0) % 8
    col = lax.broadcasted_iota(jnp.int32, (rows, t), 1)
    new_mask = (col < 8) & (col // 4 == row // 4) & (col % 4 < row % 4)
    acc, carry = _sb_tile(q_bd, kpad_ref[...].astype(BF16), vpad_ref[...].astype(BF16),
                          bias, jnp.zeros((rows, 1), F32), tri, new_mask)
    for p in reversed(range(n_pages)):
        out, carry = _sb_tile(q_bd, rows_by_heads(k_refs[p]).astype(BF16), rows_by_heads(v_refs[p]).astype(BF16),
                              bias, carry, tri, None)
        acc = acc + out

    o8 = jnp.concatenate(
        [acc[h * 8:(h + 1) * 8, h * HEAD_DIM:(h + 1) * HEAD_DIM] for h in range(N_HEADS)], axis=1)

    @pl.when(b % 2 == 0)
    def _():
        o_ref[0, 0:4, :] = o8[0:4]

    @pl.when(b % 2 == 1)
    def _():
        o_ref[0, 4:8, :] = o8[4:8]


def _attn_sample(q_s, k_heads, v_heads, row0, bias_rows, cache_k, cache_v, page_table, layer):
    dec_b, n_pages = page_table.shape
    pair_spec = pl.BlockSpec((1, 8, D_ATT), lambda b, pt: (b // 2, 0, 0))
    new_spec = pl.BlockSpec((8, N_HEADS, HEAD_DIM), lambda b, pt: (row0 // 8 + b // 2, 0, 0))

    def page_spec(p):
        return pl.BlockSpec((None, None, PAGE_SIZE, N_HEADS, HEAD_DIM),
                            lambda b, pt: (layer, pt[b, p], 0, 0, 0))

    grid_spec = pltpu.PrefetchScalarGridSpec(
        num_scalar_prefetch=1,
        grid=(dec_b,),
        in_specs=[pl.BlockSpec((8 * N_HEADS, 1), lambda b, pt: (0, 0)), pair_spec, new_spec, new_spec]
        + [page_spec(p) for p in range(n_pages)] * 2,
        out_specs=pair_spec,
        scratch_shapes=[pltpu.VMEM((PAGE_SIZE, D_ATT), F32), pltpu.VMEM((PAGE_SIZE, D_ATT), F32)],
    )
    out = pl.pallas_call(
        functools.partial(_attn_sample_kernel, n_pages),
        grid_spec=grid_spec,
        out_shape=jax.ShapeDtypeStruct((dec_b // 2, 8, D_ATT), F32),
        compiler_params=_params(("arbitrary",)),
        name="attn_sample",
    )(page_table, bias_rows, q_s.reshape(dec_b // 2, 8, D_ATT), k_heads, v_heads,
      *([cache_k] * n_pages), *([cache_v] * n_pages))
    return out.reshape(dec_b * 4, D_ATT)


def _pool_project(pooled, wp_ref, sp):
    outs = []
    for g in range(len(POOL_WINDOWS)):
        sl = slice(g * POOL_GROUP, (g + 1) * POOL_GROUP)
        outs.append(_dot(pooled[:, sl].astype(BF16), wp_ref[g]))
    return jnp.concatenate(outs, axis=1) * sp


def _rglru_inputs(xc, wa_ref, ba, wx_ref, bx, sp_lam):
    ra, ia = [], []
    for g in range(D_RNN // RNN_BLOCK):
        xg = xc[:, g * RNN_BLOCK:(g + 1) * RNN_BLOCK].astype(BF16)
        ra.append(_dot(xg, wa_ref[g]))
        ia.append(_dot(xg, wx_ref[g]))
    r = jax.nn.sigmoid(jnp.concatenate(ra, axis=1) + ba)
    i = jax.nn.sigmoid(jnp.concatenate(ia, axis=1) + bx)
    log_a = -RG_C * r * sp_lam
    a = jnp.exp(log_a)
    inp = jnp.sqrt(-jnp.tanh(log_a) * (a * a + 1.0)) * (i * xc)
    return a, inp


def _mix_prompt_kernel(rest_ref, wp_ref, sp_ref, cw_ref, cb_ref, wa_ref, ba_ref, wx_ref, bx_ref, lam_ref,
                       o_ref, h_ref, pool_ext, conv_ext, a_scr, b_scr, hs_scr, h_scr):
    nb = rest_ref.shape[0]
    tt = MIX_TILE
    step = pl.program_id(0)

    @pl.when(step == 0)
    def _():
        pool_ext[:, 0:HIST_ROWS, :] = jnp.zeros((nb, HIST_ROWS, D_POOL), F32)
        conv_ext[:, 0:CONV_HIST, :] = jnp.zeros((nb, CONV_HIST, D_RNN), F32)
        h_scr[...] = jnp.zeros_like(h_scr)

    pos = step * tt + lax.broadcasted_iota(jnp.int32, (tt, POOL_GROUP), 0)
    sp_lam = _softplus(-lam_ref[...])
    for b in range(nb):
        u = rest_ref[b, :, 0:D_POOL]
        pool_ext[b, HIST_ROWS:HIST_ROWS + tt, :] = u
        pooled = []
        for g, win in enumerate(POOL_WINDOWS):
            sl = slice(g * POOL_GROUP, (g + 1) * POOL_GROUP)
            s = u[:, sl]
            for j in range(1, win):
                s = s + pool_ext[b, HIST_ROWS - j:HIST_ROWS - j + tt, sl]
            cnt = jnp.minimum(win, pos + 1).astype(F32)
            pooled.append(s / cnt - u[:, sl])
        o_ref[b, :, 0:D_POOL] = _pool_project(jnp.concatenate(pooled, axis=1), wp_ref, sp_ref[...])
        pool_ext[b, 0:HIST_ROWS, :] = pool_ext[b, tt:tt + HIST_ROWS, :]

        ur = rest_ref[b, :, D_POOL:D_POOL + D_RNN]
        conv_ext[b, CONV_HIST:CONV_HIST + tt, :] = ur
        xc = cb_ref[...]
        for j in range(CONV_WIDTH):
            off = CONV_HIST - (CONV_WIDTH - 1) + j
            xc = xc + conv_ext[b, off:off + tt, :] * cw_ref[j:j + 1, :]
        conv_ext[b, 0:CONV_HIST, :] = conv_ext[b, tt:tt + CONV_HIST, :]
        a, inp = _rglru_inputs(xc, wa_ref, ba_ref[...], wx_ref, bx_ref[...], sp_lam)
        a_scr[b] = a
        b_scr[b] = inp

    def scan_row(t, hs):
        new = []
        for b in range(nb):
            h = a_scr[b, pl.ds(t, 1), :] * hs[b] + b_scr[b, pl.ds(t, 1), :]
            hs_scr[b, pl.ds(t, 1), :] = h
            new.append(h)
        return tuple(new)

    hs = lax.fori_loop(0, tt, scan_row, tuple(h_scr[b:b + 1, :] for b in range(nb)), unroll=8)
    for b in range(nb):
        h_scr[b:b + 1, :] = hs[b]
        o_ref[b, :, D_POOL:D_POOL + D_RNN] = hs_scr[b] * _gelu_tanh(rest_ref[b, :, D_POOL + D_RNN:D_REST])
    h_ref[...] = h_scr[0:nb, :]


def _mix_weight_specs():
    return [_resident((len(POOL_WINDOWS), POOL_GROUP, POOL_GROUP)), _resident((1, D_POOL)),
            _resident((CONV_WIDTH, D_RNN)), _resident((1, D_RNN)),
            _resident((D_RNN // RNN_BLOCK, RNN_BLOCK, RNN_BLOCK)), _resident((1, D_RNN)),
            _resident((D_RNN // RNN_BLOCK, RNN_BLOCK, RNN_BLOCK)), _resident((1, D_RNN)),
            _resident((1, D_RNN))]


def _mix_prompt(rest, mix_w, batch, seq):
    tt = MIX_TILE
    return pl.pallas_call(
        _mix_prompt_kernel,
        grid=(seq // tt,),
        in_specs=[pl.BlockSpec((batch, tt, D_REST), lambda i: (0, i, 0))] + _mix_weight_specs(),
        out_specs=[pl.BlockSpec((batch, tt, D_POOL + D_RNN), lambda i: (0, i, 0)),
                   pl.BlockSpec((batch, D_RNN), lambda i: (0, 0))],
        out_shape=[jax.ShapeDtypeStruct((batch, seq, D_POOL + D_RNN), F32),
                   jax.ShapeDtypeStruct((batch, D_RNN), F32)],
        scratch_shapes=[pltpu.VMEM((batch, HIST_ROWS + tt, D_POOL), F32),
                        pltpu.VMEM((batch, CONV_HIST + tt, D_RNN), F32),
                        pltpu.VMEM((batch, tt, D_RNN), F32),
                        pltpu.VMEM((batch, tt, D_RNN), F32),
                        pltpu.VMEM((batch, tt, D_RNN), F32),
                        pltpu.VMEM((8, D_RNN), F32)],
        compiler_params=_params(("arbitrary",)),
        name="mix_prompt",
    )(rest, *mix_w)


def _mix_sample_kernel(pos0, rest_ref, pool_ref, conv_ref, h0_ref,
                       wp_ref, sp_ref, cw_ref, cb_ref, wa_ref, ba_ref, wx_ref, bx_ref, lam_ref,
                       o_ref, h_ref):
    steps = rest_ref.shape[0]
    sp_lam = _softplus(-lam_ref[...])
    pool_rows = [pool_ref[j] for j in range(POOL_BUF)] + [rest_ref[i, :, 0:D_POOL] for i in range(steps)]
    conv_rows = ([conv_ref[j] for j in range(CONV_WIDTH - 1)]
                 + [rest_ref[i, :, D_POOL:D_POOL + D_RNN] for i in range(steps)])
    h = h0_ref[...]
    for i in range(steps):
        u = pool_rows[POOL_BUF + i]
        pooled = []
        for g, win in enumerate(POOL_WINDOWS):
            sl = slice(g * POOL_GROUP, (g + 1) * POOL_GROUP)
            s = u[:, sl]
            for j in range(1, win):
                s = s + pool_rows[POOL_BUF + i - j][:, sl]
            pooled.append(s / float(min(win, pos0 + i + 1)) - u[:, sl])
        o_ref[i, :, 0:D_POOL] = _pool_project(jnp.concatenate(pooled, axis=1), wp_ref, sp_ref[...])

        xc = cb_ref[...]
        for j in range(CONV_WIDTH):
            xc = xc + conv_rows[i + j] * cw_ref[j:j + 1, :]
        a, inp = _rglru_inputs(xc, wa_ref, ba_ref[...], wx_ref, bx_ref[...], sp_lam)
        h = a * h + inp
        o_ref[i, :, D_POOL:D_POOL + D_RNN] = h * _gelu_tanh(rest_ref[i, :, D_POOL + D_RNN:D_REST])
    h_ref[...] = h


def _mix_sample(rest_t, pool_t, conv_t, h0, mix_w, pos0):
    steps, dec_b, _ = rest_t.shape
    whole = lambda shape: pl.BlockSpec(shape, lambda i: (0,) * len(shape))
    return pl.pallas_call(
        functools.partial(_mix_sample_kernel, pos0),
        grid=(1,),
        in_specs=[whole(rest_t.shape), whole(pool_t.shape), whole(conv_t.shape), whole(h0.shape)]
        + _mix_weight_specs(),
        out_specs=[whole((steps, dec_b, D_POOL + D_RNN)), whole((dec_b, D_RNN))],
        out_shape=[jax.ShapeDtypeStruct((steps, dec_b, D_POOL + D_RNN), F32),
                   jax.ShapeDtypeStruct((dec_b, D_RNN), F32)],
        compiler_params=_params(("arbitrary",)),
        name="mix_sample",
    )(rest_t, pool_t, conv_t, h0, *mix_w)


def _out_proj_kernel(att_ref, pr_ref, x_ref, gg_ref, w_ref, gp_ref, o_ref):
    pr = pr_ref[...]
    gg = gg_ref[...]
    mixed = jnp.concatenate(
        [_normalize(att_ref[...]), _normalize(pr[:, 0:D_POOL]), _normalize(pr[:, D_POOL:])], axis=1) * gg
    y = _dot(mixed.astype(BF16), w_ref[...])
    o_ref[...] = x_ref[...] + _rms(y, gp_ref[...])


def _out_proj(att, pr, x, g_grp, w_bf, g_post):
    n = x.shape[0]
    row = lambda width: pl.BlockSpec((ROW_TILE, width), lambda i: (i, 0))
    return pl.pallas_call(
        _out_proj_kernel,
        grid=(n // ROW_TILE,),
        in_specs=[row(D_ATT), row(D_POOL + D_RNN), row(D_MODEL), _resident((1, D_MODEL)),
                  _resident((D_MODEL, D_MODEL)), _resident((1, D_MODEL))],
        out_specs=row(D_MODEL),
        out_shape=jax.ShapeDtypeStruct((n, D_MODEL), F32),
        compiler_params=_params(("arbitrary",)),
        name="out_proj",
    )(att, pr, x, g_grp, w_bf, g_post)


def _ffn_kernel(x_ref, g1_ref, wu_ref, wd_ref, g2_ref, o_ref, xn_scr, acc_scr):
    f = pl.program_id(1)

    @pl.when(f == 0)
    def _():
        xn_scr[...] = _rms(x_ref[...], g1_ref[...]).astype(BF16)
        acc_scr[...] = jnp.zeros_like(acc_scr)

    h = jnp.square(jnp.maximum(_dot(xn_scr[...], wu_ref[...]), 0.0))
    acc_scr[...] += _dot(h.astype(BF16), wd_ref[...])

    @pl.when(f == pl.num_programs(1) - 1)
    def _():
        o_ref[...] = x_ref[...] + _rms(acc_scr[...], g2_ref[...])


def _ffn(x, g_pre, wu_bf, wd_bf, g_post):
    n = x.shape[0]
    row = pl.BlockSpec((ROW_TILE, D_MODEL), lambda i, f: (i, 0))
    return pl.pallas_call(
        _ffn_kernel,
        grid=(n // ROW_TILE, D_FF // FF_TILE),
        in_specs=[row, _resident((1, D_MODEL)),
                  pl.BlockSpec((D_MODEL, FF_TILE), lambda i, f: (0, f)),
                  pl.BlockSpec((FF_TILE, D_MODEL), lambda i, f: (f, 0)),
                  _resident((1, D_MODEL))],
        out_specs=row,
        out_shape=jax.ShapeDtypeStruct((n, D_MODEL), F32),
        scratch_shapes=[pltpu.VMEM((ROW_TILE, D_MODEL), BF16), pltpu.VMEM((ROW_TILE, D_MODEL), F32)],
        compiler_params=_params(("arbitrary", "arbitrary")),
        name="ffn",
    )(x, g_pre, wu_bf, wd_bf, g_post)


def _ple_kernel(x_ref, p_ref, wp_ref, g_ref, wg_ref, o_ref):
    x = x_ref[...]
    e = _rms(_dot(p_ref[...].astype(BF16), wp_ref[...]), g_ref[...])
    o_ref[...] = x + e * jax.nn.sigmoid(_dot(x.astype(BF16), wg_ref[...]))


def _ple(x, p, wp_bf, g, wg_bf):
    n = x.shape[0]
    row = lambda width: pl.BlockSpec((ROW_TILE, width), lambda i: (i, 0))
    return pl.pallas_call(
        _ple_kernel,
        grid=(n // ROW_TILE,),
        in_specs=[row(D_MODEL), row(D_PLE), _resident((D_PLE, D_MODEL)), _resident((1, D_MODEL)),
                  _resident((D_MODEL, D_MODEL))],
        out_specs=row(D_MODEL),
        out_shape=jax.ShapeDtypeStruct((n, D_MODEL), F32),
        compiler_params=_params(("arbitrary",)),
        name="ple",
    )(x, p, wp_bf, g, wg_bf)


def kernel(x_prompt, x_sample, p_prompt, p_sample, cache_k, cache_v, page_table, state_pool, state_conv, state_h, g_pre_mix, w_in, b_sb, w_pool, s_pool, conv_w, conv_b, w_a, b_a, w_x, b_x, lam, g_grp, w_out, g_post_mix, g_pre_ffn, w_up, w_down, g_post_ffn, w_ple, g_ple, w_ple_gate):
    depth = w_in.shape[0]
    batch, seq, _ = x_prompt.shape
    dec_b, dec_t, _ = x_sample.shape
    n_p, n_s = batch * seq, dec_b * dec_t
    past_len = page_table.shape[1] * PAGE_SIZE
    assert dec_t == 4 and dec_b % 2 == 0 and n_p % ROW_TILE == 0 and n_s % ROW_TILE == 0
    assert seq % MIX_TILE == 0 and seq % ATT_TILE == 0 and past_len + dec_t > max(POOL_WINDOWS)

    x = jnp.concatenate([x_prompt.reshape(n_p, D_MODEL), x_sample.reshape(n_s, D_MODEL)], axis=0)
    row2 = lambda a: a.reshape(1, -1)
    new_p = [[] for _ in range(5)]
    new_s = [[] for _ in range(5)]
    for l in range(depth):
        q, k, v, k_heads, v_heads, rest = _in_proj(x, row2(g_pre_mix[l]), w_in[l].astype(BF16))

        att_p = _attn_prompt(q, k, v, b_sb[l], batch, seq)
        bias_rows = jnp.repeat(b_sb[l], 8).reshape(8 * N_HEADS, 1)
        att_s = _attn_sample(q[n_p:].astype(F32), k_heads, v_heads, n_p, bias_rows,
                             cache_k, cache_v, page_table, l)

        mix_w = (w_pool[l].astype(BF16), row2(s_pool[l]), conv_w[l], row2(conv_b[l]),
                 w_a[l].astype(BF16), row2(b_a[l]), w_x[l].astype(BF16), row2(b_x[l]), row2(lam[l]))
        rest_p = rest[:n_p].reshape(batch, seq, D_REST)
        pr_p, h_p = _mix_prompt(rest_p, mix_w, batch, seq)
        rest_s = rest[n_p:].reshape(dec_b, dec_t, D_REST)
        pr_s, h_s = _mix_sample(rest_s.transpose(1, 0, 2), state_pool[l].transpose(1, 0, 2),
                                state_conv[l].transpose(1, 0, 2), state_h[l], mix_w, past_len)

        att = jnp.concatenate([att_p, att_s], axis=0)
        pr = jnp.concatenate([pr_p.reshape(n_p, -1), pr_s.transpose(1, 0, 2).reshape(n_s, -1)], axis=0)
        x = _out_proj(att, pr, x, row2(g_grp[l]), w_out[l].astype(BF16), row2(g_post_mix[l]))
        x = _ffn(x, row2(g_pre_ffn[l]), w_up[l].astype(BF16), w_down[l].astype(BF16), row2(g_post_ffn[l]))
        p = jnp.concatenate([p_prompt[l].reshape(n_p, D_PLE), p_sample[l].reshape(n_s, D_PLE)], axis=0)
        x = _ple(x, p, w_ple[l].astype(BF16), row2(g_ple[l]), w_ple_gate[l].astype(BF16))

        heads = lambda a, nb: a.reshape(nb, -1, N_HEADS, HEAD_DIM)
        new_p[0].append(heads(k_heads[:n_p], batch))
        new_p[1].append(heads(v_heads[:n_p], batch))
        new_p[2].append(rest_p[:, seq - POOL_BUF:, 0:D_POOL])
        new_p[3].append(rest_p[:, seq - (CONV_WIDTH - 1):, D_POOL:D_POOL + D_RNN])
        new_p[4].append(h_p)
        new_s[0].append(heads(k_heads[n_p:], dec_b))
        new_s[1].append(heads(v_heads[n_p:], dec_b))
        new_s[2].append(jnp.concatenate([state_pool[l], rest_s[:, :, 0:D_POOL]], axis=1)[:, -POOL_BUF:])
        new_s[3].append(jnp.concatenate([state_conv[l], rest_s[:, :, D_POOL:D_POOL + D_RNN]],
                                        axis=1)[:, -(CONV_WIDTH - 1):])
        new_s[4].append(h_s)
    return (x[:n_p].reshape(batch, seq, D_MODEL), x[n_p:].reshape(dec_b, dec_t, D_MODEL),
            *[jnp.stack(a) for a in new_p], *[jnp.stack(a) for a in new_s])
```

```python
import functools

import jax
import jax.numpy as jnp
from jax import lax
from jax.experimental import pallas as pl
from jax.experimental.pallas import tpu as pltpu

F32 = jnp.float32
BF16 = jnp.bfloat16

D_MODEL = 2048
D_ATT = 1024
HEAD_DIM = 128
N_HEADS = 8
D_POOL = 512
POOL_WINDOWS = (2, 4, 8, 16)
POOL_GROUP = 128
POOL_BUF = 15
D_RNN = 512
RNN_BLOCK = 128
CONV_WIDTH = 4
RG_C = 8.0
D_FF = 8192
D_PLE = 256
PAGE_SIZE = 128
DEC_SEQ = 4
EPS = 1e-6
D_REST = D_POOL + 2 * D_RNN
D_IN = 3 * D_ATT + D_REST
SCALE = HEAD_DIM ** -0.5

VMEM_LIMIT_V7X = 56 * 1024 * 1024
MXU_DIM_V7X = 256
SUBLANES = 8
ROW_TILE = 512
FF_TILE = 1024
ATT_TILE = 512
MIX_TILE = 256
HIST_ROWS = 16
CONV_HIST = 8
PAIR_ROWS = 2 * DEC_SEQ


def _params(sem, vmem=VMEM_LIMIT_V7X):
    return pltpu.CompilerParams(dimension_semantics=sem, vmem_limit_bytes=vmem)


def _layer_block(shape, layer):
    return pl.BlockSpec((None,) + tuple(shape), lambda *_: (layer,) + (0,) * len(shape),
                        pipeline_mode=pl.Buffered(1))


def _row_block(width):
    return pl.BlockSpec((ROW_TILE, width), lambda i: (i, 0))


def _rms(x, g):
    return x * lax.rsqrt(jnp.mean(x * x, axis=-1, keepdims=True) + EPS) * g


def _normalize(x):
    return x * lax.rsqrt(jnp.mean(x * x, axis=-1, keepdims=True) + EPS)


def _dot(a, b):
    return jnp.dot(a, b, preferred_element_type=F32)


def _dot_nt(a, b):
    return lax.dot_general(a, b, (((1,), (1,)), ((), ())), preferred_element_type=F32)


def _softplus(x):
    return jnp.maximum(x, 0.0) + jnp.log1p(jnp.exp(-jnp.abs(x)))


def _gelu_tanh(x):
    c = 0.7978845608028654
    return 0.5 * x * (1.0 + jnp.tanh(c * (x + 0.044715 * (x * x * x))))


def _in_proj_kernel(n_alias, x_ref, g_ref, w_ref, *refs):
    q_ref, k_ref, v_ref, r_ref, kh_ref, vh_ref = refs[n_alias:]
    xn = _rms(x_ref[...], g_ref[...]).astype(BF16)
    q_ref[...] = _dot(xn, w_ref[:, 0:D_ATT]).astype(BF16)
    k = _dot(xn, w_ref[:, D_ATT:2 * D_ATT])
    v = _dot(xn, w_ref[:, 2 * D_ATT:3 * D_ATT])
    k_ref[...] = k.astype(BF16)
    v_ref[...] = v.astype(BF16)
    r_ref[...] = _dot(xn, w_ref[:, 3 * D_ATT:D_IN])
    for h in range(N_HEADS):
        kh_ref[:, h, :] = k[:, h * HEAD_DIM:(h + 1) * HEAD_DIM]
        vh_ref[:, h, :] = v[:, h * HEAD_DIM:(h + 1) * HEAD_DIM]


def _in_proj(x, g, w_bf, layer, kv_bufs):
    depth = w_bf.shape[0]
    n = x.shape[0]
    heads = pl.BlockSpec((None, ROW_TILE, N_HEADS, HEAD_DIM), lambda i: (layer, i, 0, 0))
    stack = jax.ShapeDtypeStruct((depth, n, N_HEADS, HEAD_DIM), F32)
    alias_in = [] if kv_bufs is None else list(kv_bufs)
    n_fixed = 3
    return pl.pallas_call(
        functools.partial(_in_proj_kernel, len(alias_in)),
        grid=(n // ROW_TILE,),
        in_specs=[_row_block(D_MODEL), _layer_block((1, D_MODEL), layer), _layer_block((D_MODEL, D_IN), layer)]
        + [pl.BlockSpec(memory_space=pl.ANY)] * len(alias_in),
        out_specs=[_row_block(D_ATT), _row_block(D_ATT), _row_block(D_ATT), _row_block(D_REST), heads, heads],
        out_shape=[jax.ShapeDtypeStruct((n, D_ATT), BF16)] * 3 + [jax.ShapeDtypeStruct((n, D_REST), F32)]
        + [stack, stack],
        input_output_aliases={n_fixed + j: 4 + j for j in range(len(alias_in))},
        compiler_params=_params(("arbitrary",)),
        name="in_proj",
    )(x, g, w_bf, *alias_in)


def _strict_upper_ones(n):
    j = lax.broadcasted_iota(jnp.int32, (n, n), 0)
    s = lax.broadcasted_iota(jnp.int32, (n, n), 1)
    return jnp.where(j > s, 1.0, 0.0).astype(BF16)


def _sb_logs(z, mask):
    l1p = jnp.log1p(jnp.exp(-jnp.abs(z)))
    log_beta = jnp.minimum(z, 0.0) - l1p
    log_stay = -jnp.maximum(z, 0.0) - l1p
    if mask is not None:
        log_stay = jnp.where(mask, log_stay, 0.0)
    return log_beta, log_stay


def _sb_weights(log_beta, log_stay, tri, carry, mask):
    n = tri.shape[0]
    chunks = log_stay.shape[1] // n
    hi = log_stay.astype(BF16)
    lo = (log_stay - hi.astype(F32)).astype(BF16)
    inner, sums = [], []
    for c in range(chunks):
        sl = slice(c * n, (c + 1) * n)
        inner.append(_dot(hi[:, sl], tri) + _dot(lo[:, sl], tri))
        sums.append(jnp.sum(log_stay[:, sl], axis=-1, keepdims=True))
    later = [None] * chunks
    for c in reversed(range(chunks)):
        later[c] = inner[c] + carry
        carry = carry + sums[c]
    w = jnp.exp(log_beta + jnp.concatenate(later, axis=1))
    if mask is not None:
        w = jnp.where(mask, w, 0.0)
    return w, carry


def _attn_prompt_kernel(bias_ref, q_ref, k_ref, v_ref, o_ref):
    t = ATT_TILE
    n_blk = q_ref.shape[0] // t
    bias = bias_ref[pl.program_id(1)]
    tri = _strict_upper_ones(MXU_DIM_V7X)
    row = lax.broadcasted_iota(jnp.int32, (t, t), 0)
    col = lax.broadcasted_iota(jnp.int32, (t, t), 1)
    diag_mask = col < row

    def tile(q_bf, k0, carry, mask):
        z = _dot_nt(q_bf, k_ref[pl.ds(k0, t), :]) * SCALE + bias
        log_beta, log_stay = _sb_logs(z, mask)
        w, carry = _sb_weights(log_beta, log_stay, tri, carry, mask)
        return _dot(w.astype(BF16), v_ref[pl.ds(k0, t), :]), carry

    def q_block(qi, _):
        q0 = pl.multiple_of(qi * t, t)
        q_bf = q_ref[pl.ds(q0, t), :]
        acc, carry = tile(q_bf, q0, jnp.zeros((t, 1), F32), diag_mask)

        def k_block(it, state):
            acc, carry = state
            out, carry = tile(q_bf, pl.multiple_of((qi - 1 - it) * t, t), carry, None)
            return acc + out, carry

        acc, _ = lax.fori_loop(0, qi, k_block, (acc, carry))
        o_ref[pl.ds(q0, t), :] = acc
        return 0

    lax.fori_loop(0, n_blk, q_block, 0)


def _attn_prompt(q, k, v, b_sb, batch, seq):
    blk = pl.BlockSpec((seq, HEAD_DIM), lambda b, h: (b, h))
    return pl.pallas_call(
        _attn_prompt_kernel,
        grid=(batch, N_HEADS),
        in_specs=[pl.BlockSpec(memory_space=pltpu.SMEM), blk, blk, blk],
        out_specs=blk,
        out_shape=jax.ShapeDtypeStruct((batch * seq, D_ATT), F32),
        compiler_params=_params(("arbitrary", "arbitrary")),
        name="attn_prompt",
    )(b_sb, q, k, v)


def _attn_sample_kernel(n_pages, pt_ref, bias_ref, q_ref, kn_ref, vn_ref, *rest):
    del pt_ref
    k_refs = rest[:n_pages]
    v_refs = rest[n_pages:2 * n_pages]
    o_ref, kpad_ref, vpad_ref = rest[2 * n_pages:]
    b = pl.program_id(0)
    rows = PAIR_ROWS * N_HEADS
    t = PAGE_SIZE

    def keys_by_heads(ref, n_keys):
        return jnp.concatenate([ref[pl.ds(h, n_keys, stride=N_HEADS), :] for h in range(N_HEADS)], axis=1)

    @pl.when(b == 0)
    def _():
        kpad_ref[...] = jnp.zeros_like(kpad_ref)
        vpad_ref[...] = jnp.zeros_like(vpad_ref)

    kpad_ref[0:PAIR_ROWS, :] = keys_by_heads(kn_ref, PAIR_ROWS)
    vpad_ref[0:PAIR_ROWS, :] = keys_by_heads(vn_ref, PAIR_ROWS)

    q_rep = jnp.concatenate([q_ref[0]] * N_HEADS, axis=0)
    r_id = lax.broadcasted_iota(jnp.int32, (rows, D_ATT), 0)
    c_id = lax.broadcasted_iota(jnp.int32, (rows, D_ATT), 1)
    q_bd = jnp.where(c_id // HEAD_DIM == r_id // PAIR_ROWS, q_rep, 0.0).astype(BF16)

    z = [_dot_nt(q_bd, keys_by_heads(k_refs[p], t).astype(BF16)) for p in range(n_pages)]
    z.append(_dot_nt(q_bd, kpad_ref[...].astype(BF16)))
    z = jnp.concatenate(z, axis=1) * SCALE + bias_ref[...]
    row = lax.broadcasted_iota(jnp.int32, z.shape, 0) % PAIR_ROWS
    col = lax.broadcasted_iota(jnp.int32, z.shape, 1) - n_pages * t
    mask = (col < 0) | ((col < PAIR_ROWS) & (col // DEC_SEQ == row // DEC_SEQ) & (col % DEC_SEQ < row % DEC_SEQ))
    log_beta, log_stay = _sb_logs(z, mask)
    w, _ = _sb_weights(log_beta, log_stay, _strict_upper_ones(t), jnp.zeros((rows, 1), F32), mask)
    w = w.astype(BF16)
    acc = _dot(w[:, n_pages * t:], vpad_ref[...].astype(BF16))
    for p in range(n_pages):
        acc = acc + _dot(w[:, p * t:(p + 1) * t], keys_by_heads(v_refs[p], t).astype(BF16))

    o8 = jnp.concatenate(
        [acc[h * PAIR_ROWS:(h + 1) * PAIR_ROWS, h * HEAD_DIM:(h + 1) * HEAD_DIM] for h in range(N_HEADS)], axis=1)

    @pl.when(b % 2 == 0)
    def _():
        o_ref[0, 0:DEC_SEQ, :] = o8[0:DEC_SEQ]

    @pl.when(b % 2 == 1)
    def _():
        o_ref[0, DEC_SEQ:PAIR_ROWS, :] = o8[DEC_SEQ:PAIR_ROWS]


def _attn_sample(q_s, k_new, v_new, layer, bias_rows, cache_k, cache_v, page_table):
    dec_b, n_pages = page_table.shape
    depth, n_pool = cache_k.shape[:2]
    flat_cache = lambda c: c.reshape(depth, n_pool, PAGE_SIZE * N_HEADS, HEAD_DIM)
    flat_new = lambda a: a.reshape(depth, dec_b * DEC_SEQ * N_HEADS, HEAD_DIM)
    pair_spec = pl.BlockSpec((1, PAIR_ROWS, D_ATT), lambda b, pt: (b // 2, 0, 0))
    new_spec = pl.BlockSpec((None, PAIR_ROWS * N_HEADS, HEAD_DIM), lambda b, pt: (layer, b // 2, 0))

    def page_spec(p):
        return pl.BlockSpec((None, None, PAGE_SIZE * N_HEADS, HEAD_DIM), lambda b, pt: (layer, pt[b, p], 0, 0))

    grid_spec = pltpu.PrefetchScalarGridSpec(
        num_scalar_prefetch=1,
        grid=(dec_b,),
        in_specs=[pl.BlockSpec((PAIR_ROWS * N_HEADS, 1), lambda b, pt: (0, 0)), pair_spec, new_spec, new_spec]
        + [page_spec(p) for p in range(n_pages)] * 2,
        out_specs=pair_spec,
        scratch_shapes=[pltpu.VMEM((PAGE_SIZE, D_ATT), F32), pltpu.VMEM((PAGE_SIZE, D_ATT), F32)],
    )
    out = pl.pallas_call(
        functools.partial(_attn_sample_kernel, n_pages),
        grid_spec=grid_spec,
        out_shape=jax.ShapeDtypeStruct((dec_b // 2, PAIR_ROWS, D_ATT), F32),
        compiler_params=_params(("arbitrary",)),
        name="attn_sample",
    )(page_table, bias_rows, q_s.reshape(dec_b // 2, PAIR_ROWS, D_ATT), flat_new(k_new), flat_new(v_new),
      *([flat_cache(cache_k)] * n_pages), *([flat_cache(cache_v)] * n_pages))
    return out.reshape(dec_b * DEC_SEQ, D_ATT)


def _pool_project(pooled, wp_ref, sp):
    outs = []
    for g in range(len(POOL_WINDOWS)):
        sl = slice(g * POOL_GROUP, (g + 1) * POOL_GROUP)
        outs.append(_dot(pooled[:, sl].astype(BF16), wp_ref[g]))
    return jnp.concatenate(outs, axis=1) * sp


def _rglru_inputs(xc, wa_ref, ba, wx_ref, bx, sp_lam):
    ra, ia = [], []
    for g in range(D_RNN // RNN_BLOCK):
        xg = xc[:, g * RNN_BLOCK:(g + 1) * RNN_BLOCK].astype(BF16)
        ra.append(_dot(xg, wa_ref[g]))
        ia.append(_dot(xg, wx_ref[g]))
    r = jax.nn.sigmoid(jnp.concatenate(ra, axis=1) + ba)
    i = jax.nn.sigmoid(jnp.concatenate(ia, axis=1) + bx)
    log_a = -RG_C * r * sp_lam
    a = jnp.exp(log_a)
    inp = jnp.sqrt(-jnp.tanh(log_a) * (a * a + 1.0)) * (i * xc)
    return a, inp


def _mix_prompt_kernel(nb, n_alias, *refs):
    rest_refs = refs[:nb]
    wp_ref, sp_ref, cw_ref, cb_ref, wa_ref, ba_ref, wx_ref, bx_ref, lam_ref = refs[nb:nb + 9]
    o_ref, h_ref, pool_ext, conv_ext, a_scr, b_scr, hs_scr, h_scr = refs[nb + 9 + n_alias:]
    tt = MIX_TILE
    step = pl.program_id(0)

    @pl.when(step == 0)
    def _():
        pool_ext[:, 0:HIST_ROWS, :] = jnp.zeros((nb, HIST_ROWS, D_POOL), F32)
        conv_ext[:, 0:CONV_HIST, :] = jnp.zeros((nb, CONV_HIST, D_RNN), F32)
        h_scr[...] = jnp.zeros_like(h_scr)

    pos = step * tt + lax.broadcasted_iota(jnp.int32, (tt, POOL_GROUP), 0)
    sp_lam = _softplus(-lam_ref[...])
    for b in range(nb):
        rest_ref = rest_refs[b]
        u = rest_ref[:, 0:D_POOL]
        pool_ext[b, HIST_ROWS:HIST_ROWS + tt, :] = u
        pooled = []
        for g, win in enumerate(POOL_WINDOWS):
            sl = slice(g * POOL_GROUP, (g + 1) * POOL_GROUP)
            s = u[:, sl]
            for j in range(1, win):
                s = s + pool_ext[b, HIST_ROWS - j:HIST_ROWS - j + tt, sl]
            cnt = jnp.minimum(win, pos + 1).astype(F32)
            pooled.append(s / cnt - u[:, sl])
        o_ref[b, :, 0:D_POOL] = _pool_project(jnp.concatenate(pooled, axis=1), wp_ref, sp_ref[...])
        pool_ext[b, 0:HIST_ROWS, :] = pool_ext[b, tt:tt + HIST_ROWS, :]

        ur = rest_ref[:, D_POOL:D_POOL + D_RNN]
        conv_ext[b, CONV_HIST:CONV_HIST + tt, :] = ur
        xc = cb_ref[...]
        for j in range(CONV_WIDTH):
            off = CONV_HIST - (CONV_WIDTH - 1) + j
            xc = xc + conv_ext[b, off:off + tt, :] * cw_ref[j:j + 1, :]
        conv_ext[b, 0:CONV_HIST, :] = conv_ext[b, tt:tt + CONV_HIST, :]
        a, inp = _rglru_inputs(xc, wa_ref, ba_ref[...], wx_ref, bx_ref[...], sp_lam)
        a_scr[b] = a
        b_scr[b] = inp

    def scan_row(t, hs):
        new = []
        for b in range(nb):
            h = a_scr[b, pl.ds(t, 1), :] * hs[b] + b_scr[b, pl.ds(t, 1), :]
            hs_scr[b, pl.ds(t, 1), :] = h
            new.append(h)
        return tuple(new)

    hs = lax.fori_loop(0, tt, scan_row, tuple(h_scr[b:b + 1, :] for b in range(nb)), unroll=8)
    for b in range(nb):
        h_scr[b:b + 1, :] = hs[b]
        o_ref[b, :, D_POOL:D_POOL + D_RNN] = hs_scr[b] * _gelu_tanh(rest_refs[b][:, D_POOL + D_RNN:D_REST])
    h_ref[0] = h_scr[0:nb, :]


def _mix_weight_specs(layer):
    n_blk = D_RNN // RNN_BLOCK
    return [_layer_block((len(POOL_WINDOWS), POOL_GROUP, POOL_GROUP), layer), _layer_block((1, D_POOL), layer),
            _layer_block((CONV_WIDTH, D_RNN), layer), _layer_block((1, D_RNN), layer),
            _layer_block((n_blk, RNN_BLOCK, RNN_BLOCK), layer), _layer_block((1, D_RNN), layer),
            _layer_block((n_blk, RNN_BLOCK, RNN_BLOCK), layer), _layer_block((1, D_RNN), layer),
            _layer_block((1, D_RNN), layer)]


def _mix_prompt(rest, mix_w, layer, batch, seq, h_buf):
    tt = MIX_TILE
    depth = mix_w[0].shape[0]
    steps = seq // tt
    alias_in = [] if h_buf is None else [h_buf]
    rest_specs = [pl.BlockSpec((tt, D_REST), functools.partial(lambda b, i: (b * steps + i, 0), b))
                  for b in range(batch)]
    return pl.pallas_call(
        functools.partial(_mix_prompt_kernel, batch, len(alias_in)),
        grid=(steps,),
        in_specs=rest_specs + _mix_weight_specs(layer) + [pl.BlockSpec(memory_space=pl.ANY)] * len(alias_in),
        out_specs=[pl.BlockSpec((batch, tt, D_POOL + D_RNN), lambda i: (0, i, 0)),
                   pl.BlockSpec((1, batch, D_RNN), lambda i: (layer, 0, 0))],
        out_shape=[jax.ShapeDtypeStruct((batch, seq, D_POOL + D_RNN), F32),
                   jax.ShapeDtypeStruct((depth, batch, D_RNN), F32)],
        scratch_shapes=[pltpu.VMEM((batch, HIST_ROWS + tt, D_POOL), F32),
                        pltpu.VMEM((batch, CONV_HIST + tt, D_RNN), F32),
                        pltpu.VMEM((batch, tt, D_RNN), F32),
                        pltpu.VMEM((batch, tt, D_RNN), F32),
                        pltpu.VMEM((batch, tt, D_RNN), F32),
                        pltpu.VMEM((SUBLANES, D_RNN), F32)],
        input_output_aliases={batch + len(mix_w) + j: 1 + j for j in range(len(alias_in))},
        compiler_params=_params(("arbitrary",)),
        name="mix_prompt",
    )(*([rest] * batch), *mix_w, *alias_in)


def _mix_sample_kernel(pos0, n_alias, rest_ref, pool_ref, conv_ref, h0_ref,
                       wp_ref, sp_ref, cw_ref, cb_ref, wa_ref, ba_ref, wx_ref, bx_ref, lam_ref, *refs):
    o_ref, h_ref = refs[n_alias:]
    steps = rest_ref.shape[0]
    sp_lam = _softplus(-lam_ref[...])
    pool_rows = [pool_ref[j] for j in range(POOL_BUF)] + [rest_ref[i, :, 0:D_POOL] for i in range(steps)]
    conv_rows = ([conv_ref[j] for j in range(CONV_WIDTH - 1)]
                 + [rest_ref[i, :, D_POOL:D_POOL + D_RNN] for i in range(steps)])
    h = h0_ref[...]
    for i in range(steps):
        u = pool_rows[POOL_BUF + i]
        pooled = []
        for g, win in enumerate(POOL_WINDOWS):
            sl = slice(g * POOL_GROUP, (g + 1) * POOL_GROUP)
            s = u[:, sl]
            for j in range(1, win):
                s = s + pool_rows[POOL_BUF + i - j][:, sl]
            pooled.append(s / float(min(win, pos0 + i + 1)) - u[:, sl])
        o_ref[i, :, 0:D_POOL] = _pool_project(jnp.concatenate(pooled, axis=1), wp_ref, sp_ref[...])

        xc = cb_ref[...]
        for j in range(CONV_WIDTH):
            xc = xc + conv_rows[i + j] * cw_ref[j:j + 1, :]
        a, inp = _rglru_inputs(xc, wa_ref, ba_ref[...], wx_ref, bx_ref[...], sp_lam)
        h = a * h + inp
        o_ref[i, :, D_POOL:D_POOL + D_RNN] = h * _gelu_tanh(rest_ref[i, :, D_POOL + D_RNN:D_REST])
    h_ref[0] = h


def _mix_sample(rest_t, pool_t, conv_t, state_h, mix_w, layer, pos0, h_buf):
    steps, dec_b, _ = rest_t.shape
    depth = state_h.shape[0]
    whole = lambda shape: pl.BlockSpec(shape, lambda i: (0,) * len(shape))
    alias_in = [] if h_buf is None else [h_buf]
    n_in = 4 + len(mix_w)
    return pl.pallas_call(
        functools.partial(_mix_sample_kernel, pos0, len(alias_in)),
        grid=(1,),
        in_specs=[whole(rest_t.shape), whole(pool_t.shape), whole(conv_t.shape),
                  pl.BlockSpec((None, dec_b, D_RNN), lambda i: (layer, 0, 0))]
        + _mix_weight_specs(layer) + [pl.BlockSpec(memory_space=pl.ANY)] * len(alias_in),
        out_specs=[whole((steps, dec_b, D_POOL + D_RNN)),
                   pl.BlockSpec((1, dec_b, D_RNN), lambda i: (layer, 0, 0))],
        out_shape=[jax.ShapeDtypeStruct((steps, dec_b, D_POOL + D_RNN), F32),
                   jax.ShapeDtypeStruct((depth, dec_b, D_RNN), F32)],
        input_output_aliases={n_in + j: 1 + j for j in range(len(alias_in))},
        compiler_params=_params(("arbitrary",)),
        name="mix_sample",
    )(rest_t, pool_t, conv_t, state_h, *mix_w, *alias_in)


def _out_proj_kernel(att_ref, pr_ref, x_ref, gg_ref, w_ref, gp_ref, o_ref):
    pr = pr_ref[...]
    mixed = jnp.concatenate(
        [_normalize(att_ref[...]), _normalize(pr[:, 0:D_POOL]), _normalize(pr[:, D_POOL:])],
        axis=1) * gg_ref[...]
    y = _dot(mixed.astype(BF16), w_ref[...])
    o_ref[...] = x_ref[...] + _rms(y, gp_ref[...])


def _out_proj(att, pr, x, g_grp, w_bf, g_post, layer):
    n = x.shape[0]
    return pl.pallas_call(
        _out_proj_kernel,
        grid=(n // ROW_TILE,),
        in_specs=[_row_block(D_ATT), _row_block(D_POOL + D_RNN), _row_block(D_MODEL),
                  _layer_block((1, D_MODEL), layer), _layer_block((D_MODEL, D_MODEL), layer),
                  _layer_block((1, D_MODEL), layer)],
        out_specs=_row_block(D_MODEL),
        out_shape=jax.ShapeDtypeStruct((n, D_MODEL), F32),
        compiler_params=_params(("arbitrary",)),
        name="out_proj",
    )(att, pr, x, g_grp, w_bf, g_post)


def _ffn_kernel(x_ref, g1_ref, wu_ref, wd_ref, g2_ref, o_ref, xn_scr, acc_scr):
    f = pl.program_id(1)

    @pl.when(f == 0)
    def _():
        xn_scr[...] = _rms(x_ref[...], g1_ref[...]).astype(BF16)
        acc_scr[...] = jnp.zeros_like(acc_scr)

    h = jnp.square(jnp.maximum(_dot(xn_scr[...], wu_ref[...]), 0.0))
    acc_scr[...] += _dot(h.astype(BF16), wd_ref[...])

    @pl.when(f == pl.num_programs(1) - 1)
    def _():
        o_ref[...] = x_ref[...] + _rms(acc_scr[...], g2_ref[...])


def _ffn(x, g_pre, wu_bf, wd_bf, g_post, layer):
    n = x.shape[0]
    row = pl.BlockSpec((ROW_TILE, D_MODEL), lambda i, f: (i, 0))
    return pl.pallas_call(
        _ffn_kernel,
        grid=(n // ROW_TILE, D_FF // FF_TILE),
        in_specs=[row, _layer_block((1, D_MODEL), layer),
                  pl.BlockSpec((None, D_MODEL, FF_TILE), lambda i, f: (layer, 0, f)),
                  pl.BlockSpec((None, FF_TILE, D_MODEL), lambda i, f: (layer, f, 0)),
                  _layer_block((1, D_MODEL), layer)],
        out_specs=row,
        out_shape=jax.ShapeDtypeStruct((n, D_MODEL), F32),
        scratch_shapes=[pltpu.VMEM((ROW_TILE, D_MODEL), BF16), pltpu.VMEM((ROW_TILE, D_MODEL), F32)],
        compiler_params=_params(("arbitrary", "arbitrary")),
        name="ffn",
    )(x, g_pre, wu_bf, wd_bf, g_post)


def _ple_kernel(x_ref, p_ref, wp_ref, g_ref, wg_ref, o_ref):
    x = x_ref[...]
    e = _rms(_dot(p_ref[...].astype(BF16), wp_ref[...]), g_ref[...])
    o_ref[...] = x + e * jax.nn.sigmoid(_dot(x.astype(BF16), wg_ref[...]))


def _ple(x, p, wp_bf, g, wg_bf, layer):
    n = x.shape[0]
    return pl.pallas_call(
        _ple_kernel,
        grid=(n // ROW_TILE,),
        in_specs=[_row_block(D_MODEL), pl.BlockSpec((None, ROW_TILE, D_PLE), lambda i: (layer, i, 0)),
                  _layer_block((D_PLE, D_MODEL), layer), _layer_block((1, D_MODEL), layer),
                  _layer_block((D_MODEL, D_MODEL), layer)],
        out_specs=_row_block(D_MODEL),
        out_shape=jax.ShapeDtypeStruct((n, D_MODEL), F32),
        compiler_params=_params(("arbitrary",)),
        name="ple",
    )(x, p, wp_bf, g, wg_bf)


def kernel(x_prompt, x_sample, p_prompt, p_sample, cache_k, cache_v, page_table, state_pool, state_conv, state_h, g_pre_mix, w_in, b_sb, w_pool, s_pool, conv_w, conv_b, w_a, b_a, w_x, b_x, lam, g_grp, w_out, g_post_mix, g_pre_ffn, w_up, w_down, g_post_ffn, w_ple, g_ple, w_ple_gate):
    depth = w_in.shape[0]
    batch, seq, _ = x_prompt.shape
    dec_b, dec_t, _ = x_sample.shape
    n_p, n_s = batch * seq, dec_b * dec_t
    past_len = page_table.shape[1] * PAGE_SIZE
    assert dec_t == DEC_SEQ and dec_b % 2 == 0 and n_p % ROW_TILE == 0 and n_s % ROW_TILE == 0
    assert seq % MIX_TILE == 0 and seq % ATT_TILE == 0 and past_len + dec_t > max(POOL_WINDOWS)

    bf = lambda a: a.astype(BF16)
    vec = lambda a: a.reshape(depth, 1, -1)
    w_in_bf, w_out_bf, w_up_bf, w_down_bf = bf(w_in), bf(w_out), bf(w_up), bf(w_down)
    w_ple_bf, w_gate_bf = bf(w_ple), bf(w_ple_gate)
    mix_w = (bf(w_pool), vec(s_pool), conv_w, vec(conv_b), bf(w_a), vec(b_a), bf(w_x), vec(b_x), vec(lam))
    g_pre_mix, g_grp, g_post_mix, g_pre_ffn, g_post_ffn, g_ple = map(
        vec, (g_pre_mix, g_grp, g_post_mix, g_pre_ffn, g_post_ffn, g_ple))
    pp = p_prompt.reshape(depth, n_p, D_PLE)
    ps = p_sample.reshape(depth, n_s, D_PLE)
    pool_t = state_pool.transpose(0, 2, 1, 3)
    conv_t = state_conv.transpose(0, 2, 1, 3)

    xp = x_prompt.reshape(n_p, D_MODEL)
    xs = x_sample.reshape(n_s, D_MODEL)
    kv_p = kv_s = hp_buf = hs_buf = None
    pool_p, conv_p, pool_s, conv_s = [], [], [], []

    def dense_tail(att, pr, x, p, l):
        x = _out_proj(att, pr, x, g_grp, w_out_bf, g_post_mix, l)
        x = _ffn(x, g_pre_ffn, w_up_bf, w_down_bf, g_post_ffn, l)
        return _ple(x, p, w_ple_bf, g_ple, w_gate_bf, l)

    for l in range(depth):
        q, k, v, rest, *kv_p = _in_proj(xp, g_pre_mix, w_in_bf, l, kv_p)
        att = _attn_prompt(q, k, v, b_sb[l], batch, seq)
        pr, hp_buf = _mix_prompt(rest, mix_w, l, batch, seq, hp_buf)
        xp = dense_tail(att, pr.reshape(n_p, -1), xp, pp, l)
        tail = lambda rows, lo: jnp.stack(
            [rest[(b + 1) * seq - rows:(b + 1) * seq, lo:lo + D_POOL] for b in range(batch)])
        pool_p.append(tail(POOL_BUF, 0))
        conv_p.append(tail(CONV_WIDTH - 1, D_POOL))

        q, _, _, rest, *kv_s = _in_proj(xs, g_pre_mix, w_in_bf, l, kv_s)
        bias_rows = jnp.repeat(b_sb[l], PAIR_ROWS).reshape(PAIR_ROWS * N_HEADS, 1)
        att = _attn_sample(q.astype(F32), kv_s[0], kv_s[1], l, bias_rows, cache_k, cache_v, page_table)
        rest = rest.reshape(dec_b, dec_t, D_REST)
        pr, hs_buf = _mix_sample(rest.transpose(1, 0, 2), pool_t[l], conv_t[l], state_h,
                                 mix_w, l, past_len, hs_buf)
        xs = dense_tail(att, pr.transpose(1, 0, 2).reshape(n_s, -1), xs, ps, l)
        pool_s.append(jnp.concatenate([state_pool[l], rest[:, :, 0:D_POOL]], axis=1)[:, -POOL_BUF:])
        conv_s.append(jnp.concatenate([state_conv[l], rest[:, :, D_POOL:D_POOL + D_RNN]],
                                      axis=1)[:, -(CONV_WIDTH - 1):])

    kp, vp = kv_p
    ks, vs = kv_s
    heads = lambda a, nb: a.reshape(depth, nb, -1, N_HEADS, HEAD_DIM)
    return (xp.reshape(batch, seq, D_MODEL), xs.reshape(dec_b, dec_t, D_MODEL),
            heads(kp, batch), heads(vp, batch), jnp.stack(pool_p), jnp.stack(conv_p), hp_buf,
            heads(ks, dec_b), heads(vs, dec_b), jnp.stack(pool_s), jnp.stack(conv_s), hs_buf)
```

```python
import functools

import jax
import jax.numpy as jnp
from jax import lax
from jax.experimental import pallas as pl
from jax.experimental.pallas import tpu as pltpu

F32 = jnp.float32
BF16 = jnp.bfloat16

D_MODEL = 2048
D_ATT = 1024
HEAD_DIM = 128
N_HEADS = 8
D_POOL = 512
POOL_WINDOWS = (2, 4, 8, 16)
POOL_GROUP = 128
POOL_BUF = 15
D_RNN = 512
RNN_BLOCK = 128
CONV_WIDTH = 4
RG_C = 8.0
D_FF = 8192
D_PLE = 256
PAGE_SIZE = 128
DEC_SEQ = 4
EPS = 1e-6
D_REST = D_POOL + 2 * D_RNN
D_IN = 3 * D_ATT + D_REST
SCALE = HEAD_DIM ** -0.5

VMEM_LIMIT_V7X = 56 * 1024 * 1024
MXU_DIM_V7X = 256
SUBLANES = 8
ROW_TILE = 512
FF_TILE = 1024
ATT_TILE = 512
MIX_TILE = 256
HIST_ROWS = 16
CONV_HIST = 8
PAIR_ROWS = 2 * DEC_SEQ


def _params(sem, vmem=VMEM_LIMIT_V7X):
    return pltpu.CompilerParams(dimension_semantics=sem, vmem_limit_bytes=vmem)


def _layer_block(shape, layer):
    return pl.BlockSpec((None,) + tuple(shape), lambda *_: (layer,) + (0,) * len(shape),
                        pipeline_mode=pl.Buffered(1))


def _row_block(width):
    return pl.BlockSpec((ROW_TILE, width), lambda i: (i, 0))


def _rms(x, g):
    return x * lax.rsqrt(jnp.mean(x * x, axis=-1, keepdims=True) + EPS) * g


def _normalize(x):
    return x * lax.rsqrt(jnp.mean(x * x, axis=-1, keepdims=True) + EPS)


def _dot(a, b):
    return jnp.dot(a, b, preferred_element_type=F32)


def _dot_nt(a, b):
    return lax.dot_general(a, b, (((1,), (1,)), ((), ())), preferred_element_type=F32)


def _softplus(x):
    return jnp.maximum(x, 0.0) + jnp.log1p(jnp.exp(-jnp.abs(x)))


def _gelu_tanh(x):
    c = 0.7978845608028654
    return 0.5 * x * (1.0 + jnp.tanh(c * (x + 0.044715 * (x * x * x))))


def _in_proj_kernel(n_alias, x_ref, g_ref, w_ref, *refs):
    q_ref, k_ref, v_ref, r_ref, kh_ref, vh_ref = refs[n_alias:]
    xn = _rms(x_ref[...], g_ref[...]).astype(BF16)
    q_ref[...] = _dot(xn, w_ref[:, 0:D_ATT]).astype(BF16)
    k = _dot(xn, w_ref[:, D_ATT:2 * D_ATT])
    v = _dot(xn, w_ref[:, 2 * D_ATT:3 * D_ATT])
    k_ref[...] = k.astype(BF16)
    v_ref[...] = v.astype(BF16)
    r_ref[...] = _dot(xn, w_ref[:, 3 * D_ATT:D_IN])
    for h in range(N_HEADS):
        kh_ref[pl.ds(h, ROW_TILE, stride=N_HEADS), :] = k[:, h * HEAD_DIM:(h + 1) * HEAD_DIM]
        vh_ref[pl.ds(h, ROW_TILE, stride=N_HEADS), :] = v[:, h * HEAD_DIM:(h + 1) * HEAD_DIM]


def _in_proj(x, g, w_bf, layer, kv_bufs):
    depth = w_bf.shape[0]
    n = x.shape[0]
    heads = pl.BlockSpec((None, ROW_TILE * N_HEADS, HEAD_DIM), lambda i: (layer, i, 0))
    stack = jax.ShapeDtypeStruct((depth, n * N_HEADS, HEAD_DIM), F32)
    alias_in = [] if kv_bufs is None else list(kv_bufs)
    n_fixed = 3
    return pl.pallas_call(
        functools.partial(_in_proj_kernel, len(alias_in)),
        grid=(n // ROW_TILE,),
        in_specs=[_row_block(D_MODEL), _layer_block((1, D_MODEL), layer), _layer_block((D_MODEL, D_IN), layer)]
        + [pl.BlockSpec(memory_space=pl.ANY)] * len(alias_in),
        out_specs=[_row_block(D_ATT), _row_block(D_ATT), _row_block(D_ATT), _row_block(D_REST), heads, heads],
        out_shape=[jax.ShapeDtypeStruct((n, D_ATT), BF16)] * 3 + [jax.ShapeDtypeStruct((n, D_REST), F32)]
        + [stack, stack],
        input_output_aliases={n_fixed + j: 4 + j for j in range(len(alias_in))},
        compiler_params=_params(("arbitrary",)),
        name="in_proj",
    )(x, g, w_bf, *alias_in)


def _strict_upper_ones(n):
    j = lax.broadcasted_iota(jnp.int32, (n, n), 0)
    s = lax.broadcasted_iota(jnp.int32, (n, n), 1)
    return jnp.where(j > s, 1.0, 0.0).astype(BF16)


def _sb_logs(z, mask):
    log_beta = jnp.minimum(z, 0.0) - jnp.log(1.0 + jnp.exp(-jnp.abs(z)))
    log_stay = log_beta - z
    if mask is not None:
        log_stay = jnp.where(mask, log_stay, 0.0)
    return log_beta, log_stay


def _sb_weights(log_beta, log_stay, tri, carry, mask):
    n = tri.shape[0]
    chunks = log_stay.shape[1] // n
    hi = log_stay.astype(BF16)
    lo = (log_stay - hi.astype(F32)).astype(BF16)
    tri2 = jnp.concatenate([tri, tri], axis=0)
    inner, sums = [], []
    for c in range(chunks):
        sl = slice(c * n, (c + 1) * n)
        inner.append(_dot(jnp.concatenate([hi[:, sl], lo[:, sl]], axis=1), tri2))
        sums.append(jnp.sum(log_stay[:, sl], axis=-1, keepdims=True))
    later = [None] * chunks
    for c in reversed(range(chunks)):
        later[c] = inner[c] + carry
        carry = carry + sums[c]
    w = jnp.exp(log_beta + jnp.concatenate(later, axis=1))
    if mask is not None:
        w = jnp.where(mask, w, 0.0)
    return w, carry


def _attn_prompt_kernel(bias_ref, q_ref, k_ref, v_ref, wu_ref, wd_ref, o_ref, wu_bf_ref, wd_bf_ref):
    wu_bf_ref[...] = wu_ref[...].astype(BF16)
    wd_bf_ref[...] = wd_ref[...].astype(BF16)
    t = ATT_TILE
    n_blk = q_ref.shape[0] // t
    bias = bias_ref[pl.program_id(1)]
    tri = _strict_upper_ones(MXU_DIM_V7X)
    row = lax.broadcasted_iota(jnp.int32, (t, t), 0)
    col = lax.broadcasted_iota(jnp.int32, (t, t), 1)
    diag_mask = col < row

    def tile(q_bf, k0, carry, mask):
        z = _dot_nt(q_bf, k_ref[pl.ds(k0, t), :]) * SCALE + bias
        log_beta, log_stay = _sb_logs(z, mask)
        w, carry = _sb_weights(log_beta, log_stay, tri, carry, mask)
        return _dot(w.astype(BF16), v_ref[pl.ds(k0, t), :]), carry

    def q_block(qi, _):
        q0 = pl.multiple_of(qi * t, t)
        q_bf = q_ref[pl.ds(q0, t), :]
        acc, carry = tile(q_bf, q0, jnp.zeros((t, 1), F32), diag_mask)

        def k_block(it, state):
            acc, carry = state
            out, carry = tile(q_bf, pl.multiple_of((qi - 1 - it) * t, t), carry, None)
            return acc + out, carry

        acc, _ = lax.fori_loop(0, qi, k_block, (acc, carry))
        o_ref[pl.ds(q0, t), :] = acc
        return 0

    lax.fori_loop(0, n_blk, q_block, 0)


def _attn_prompt(q, k, v, b_sb, w_up, w_down, layer, batch, seq):
    blk = pl.BlockSpec((seq, HEAD_DIM), lambda b, h: (b, h))
    steps = batch * N_HEADS
    up_rows, down_rows = D_MODEL // steps, D_FF // steps
    assert up_rows % 16 == 0 and down_rows % 16 == 0
    return pl.pallas_call(
        _attn_prompt_kernel,
        grid=(batch, N_HEADS),
        in_specs=[pl.BlockSpec(memory_space=pltpu.SMEM), blk, blk, blk,
                  pl.BlockSpec((None, up_rows, D_FF), lambda b, h: (layer, b * N_HEADS + h, 0)),
                  pl.BlockSpec((None, down_rows, D_MODEL), lambda b, h: (layer, b * N_HEADS + h, 0))],
        out_specs=[blk, pl.BlockSpec((up_rows, D_FF), lambda b, h: (b * N_HEADS + h, 0)),
                   pl.BlockSpec((down_rows, D_MODEL), lambda b, h: (b * N_HEADS + h, 0))],
        out_shape=[jax.ShapeDtypeStruct((batch * seq, D_ATT), F32),
                   jax.ShapeDtypeStruct((D_MODEL, D_FF), BF16), jax.ShapeDtypeStruct((D_FF, D_MODEL), BF16)],
        compiler_params=_params(("arbitrary", "arbitrary")),
        name="attn_prompt",
    )(b_sb, q, k, v, w_up, w_down)


def _attn_sample_kernel(n_pages, pt_ref, bias_ref, q_ref, kn_ref, vn_ref, *rest):
    del pt_ref
    k_refs = rest[:n_pages]
    v_refs = rest[n_pages:2 * n_pages]
    o_ref, kpad_ref, vpad_ref = rest[2 * n_pages:]
    b = pl.program_id(0)
    rows = PAIR_ROWS * N_HEADS
    t = PAGE_SIZE

    def keys_by_heads(ref, n_keys):
        return jnp.concatenate([ref[pl.ds(h, n_keys, stride=N_HEADS), :] for h in range(N_HEADS)], axis=1)

    @pl.when(b == 0)
    def _():
        kpad_ref[...] = jnp.zeros_like(kpad_ref)
        vpad_ref[...] = jnp.zeros_like(vpad_ref)

    kpad_ref[0:PAIR_ROWS, :] = keys_by_heads(kn_ref, PAIR_ROWS)
    vpad_ref[0:PAIR_ROWS, :] = keys_by_heads(vn_ref, PAIR_ROWS)

    q_rep = jnp.concatenate([q_ref[0]] * N_HEADS, axis=0)
    r_id = lax.broadcasted_iota(jnp.int32, (rows, D_ATT), 0)
    c_id = lax.broadcasted_iota(jnp.int32, (rows, D_ATT), 1)
    q_bd = jnp.where(c_id // HEAD_DIM == r_id // PAIR_ROWS, q_rep, 0.0).astype(BF16)

    z = [_dot_nt(q_bd, keys_by_heads(k_refs[p], t).astype(BF16)) for p in range(n_pages)]
    z.append(_dot_nt(q_bd, kpad_ref[...].astype(BF16)))
    z = jnp.concatenate(z, axis=1) * SCALE + bias_ref[...]
    row = lax.broadcasted_iota(jnp.int32, z.shape, 0) % PAIR_ROWS
    col = lax.broadcasted_iota(jnp.int32, z.shape, 1) - n_pages * t
    mask = (col < 0) | ((col < PAIR_ROWS) & (col // DEC_SEQ == row // DEC_SEQ) & (col % DEC_SEQ < row % DEC_SEQ))
    log_beta, log_stay = _sb_logs(z, mask)
    w, _ = _sb_weights(log_beta, log_stay, _strict_upper_ones(t), jnp.zeros((rows, 1), F32), mask)
    w = w.astype(BF16)
    acc = _dot(w[:, n_pages * t:], vpad_ref[...].astype(BF16))
    for p in range(n_pages):
        acc = acc + _dot(w[:, p * t:(p + 1) * t], keys_by_heads(v_refs[p], t).astype(BF16))

    o8 = jnp.concatenate(
        [acc[h * PAIR_ROWS:(h + 1) * PAIR_ROWS, h * HEAD_DIM:(h + 1) * HEAD_DIM] for h in range(N_HEADS)], axis=1)

    @pl.when(b % 2 == 0)
    def _():
        o_ref[0, 0:DEC_SEQ, :] = o8[0:DEC_SEQ]

    @pl.when(b % 2 == 1)
    def _():
        o_ref[0, DEC_SEQ:PAIR_ROWS, :] = o8[DEC_SEQ:PAIR_ROWS]


def _attn_sample(q_s, k_new, v_new, layer, bias_rows, cache_k, cache_v, page_table):
    dec_b, n_pages = page_table.shape
    depth, n_pool = cache_k.shape[:2]
    flat_cache = lambda c: c.reshape(depth, n_pool, PAGE_SIZE * N_HEADS, HEAD_DIM)
    pair_spec = pl.BlockSpec((1, PAIR_ROWS, D_ATT), lambda b, pt: (b // 2, 0, 0))
    new_spec = pl.BlockSpec((None, PAIR_ROWS * N_HEADS, HEAD_DIM), lambda b, pt: (layer, b // 2, 0))

    def page_spec(p):
        return pl.BlockSpec((None, None, PAGE_SIZE * N_HEADS, HEAD_DIM), lambda b, pt: (layer, pt[b, p], 0, 0))

    grid_spec = pltpu.PrefetchScalarGridSpec(
        num_scalar_prefetch=1,
        grid=(dec_b,),
        in_specs=[pl.BlockSpec((PAIR_ROWS * N_HEADS, 1), lambda b, pt: (0, 0)), pair_spec, new_spec, new_spec]
        + [page_spec(p) for p in range(n_pages)] * 2,
        out_specs=pair_spec,
        scratch_shapes=[pltpu.VMEM((PAGE_SIZE, D_ATT), F32), pltpu.VMEM((PAGE_SIZE, D_ATT), F32)],
    )
    out = pl.pallas_call(
        functools.partial(_attn_sample_kernel, n_pages),
        grid_spec=grid_spec,
        out_shape=jax.ShapeDtypeStruct((dec_b // 2, PAIR_ROWS, D_ATT), F32),
        compiler_params=_params(("arbitrary",)),
        name="attn_sample",
    )(page_table, bias_rows, q_s.reshape(dec_b // 2, PAIR_ROWS, D_ATT), k_new, v_new,
      *([flat_cache(cache_k)] * n_pages), *([flat_cache(cache_v)] * n_pages))
    return out.reshape(dec_b * DEC_SEQ, D_ATT)


def _pool_project(pooled, wp_ref, sp):
    outs = []
    for g in range(len(POOL_WINDOWS)):
        sl = slice(g * POOL_GROUP, (g + 1) * POOL_GROUP)
        outs.append(_dot(pooled[:, sl].astype(BF16), wp_ref[g]))
    return jnp.concatenate(outs, axis=1) * sp


def _rglru_inputs(xc, wa_ref, ba, wx_ref, bx, sp_lam):
    ra, ia = [], []
    for g in range(D_RNN // RNN_BLOCK):
        xg = xc[:, g * RNN_BLOCK:(g + 1) * RNN_BLOCK].astype(BF16)
        ra.append(_dot(xg, wa_ref[g]))
        ia.append(_dot(xg, wx_ref[g]))
    r = jax.nn.sigmoid(jnp.concatenate(ra, axis=1) + ba)
    i = jax.nn.sigmoid(jnp.concatenate(ia, axis=1) + bx)
    log_a = -RG_C * r * sp_lam
    a = jnp.exp(log_a)
    inp = jnp.sqrt(-jnp.tanh(log_a) * (a * a + 1.0)) * (i * xc)
    return a, inp


def _mix_prompt_kernel(nb, n_alias, *refs):
    rest_refs = refs[:nb]
    wp_ref, sp_ref, cw_ref, cb_ref, wa_ref, ba_ref, wx_ref, bx_ref, lam_ref = refs[nb:nb + 9]
    o_ref, h_ref, pool_ext, conv_ext, a_scr, b_scr, hs_scr, h_scr = refs[nb + 9 + n_alias:]
    tt = MIX_TILE
    step = pl.program_id(0)

    @pl.when(step == 0)
    def _():
        pool_ext[:, 0:HIST_ROWS, :] = jnp.zeros((nb, HIST_ROWS, D_POOL), F32)
        conv_ext[:, 0:CONV_HIST, :] = jnp.zeros((nb, CONV_HIST, D_RNN), F32)
        h_scr[...] = jnp.zeros_like(h_scr)

    pos = step * tt + lax.broadcasted_iota(jnp.int32, (tt, POOL_GROUP), 0)
    sp_lam = _softplus(-lam_ref[...])
    for b in range(nb):
        rest_ref = rest_refs[b]
        u = rest_ref[:, 0:D_POOL]
        pool_ext[b, HIST_ROWS:HIST_ROWS + tt, :] = u
        pooled = []
        for g, win in enumerate(POOL_WINDOWS):
            sl = slice(g * POOL_GROUP, (g + 1) * POOL_GROUP)
            s = u[:, sl]
            for j in range(1, win):
                s = s + pool_ext[b, HIST_ROWS - j:HIST_ROWS - j + tt, sl]
            cnt = jnp.minimum(win, pos + 1).astype(F32)
            pooled.append(s / cnt - u[:, sl])
        o_ref[b, :, 0:D_POOL] = _pool_project(jnp.concatenate(pooled, axis=1), wp_ref, sp_ref[...])
        pool_ext[b, 0:HIST_ROWS, :] = pool_ext[b, tt:tt + HIST_ROWS, :]

        ur = rest_ref[:, D_POOL:D_POOL + D_RNN]
        conv_ext[b, CONV_HIST:CONV_HIST + tt, :] = ur
        xc = cb_ref[...]
        for j in range(CONV_WIDTH):
            off = CONV_HIST - (CONV_WIDTH - 1) + j
            xc = xc + conv_ext[b, off:off + tt, :] * cw_ref[j:j + 1, :]
        conv_ext[b, 0:CONV_HIST, :] = conv_ext[b, tt:tt + CONV_HIST, :]
        a, inp = _rglru_inputs(xc, wa_ref, ba_ref[...], wx_ref, bx_ref[...], sp_lam)
        a_scr[b] = a
        b_scr[b] = inp

    def scan_row(t, hs):
        new = []
        for b in range(nb):
            h = a_scr[b, pl.ds(t, 1), :] * hs[b] + b_scr[b, pl.ds(t, 1), :]
            hs_scr[b, pl.ds(t, 1), :] = h
            new.append(h)
        return tuple(new)

    hs = lax.fori_loop(0, tt, scan_row, tuple(h_scr[b:b + 1, :] for b in range(nb)), unroll=8)
    for b in range(nb):
        h_scr[b:b + 1, :] = hs[b]
        o_ref[b, :, D_POOL:D_POOL + D_RNN] = hs_scr[b] * _gelu_tanh(rest_refs[b][:, D_POOL + D_RNN:D_REST])
    h_ref[0] = h_scr[0:nb, :]


def _mix_weight_specs(layer):
    n_blk = D_RNN // RNN_BLOCK
    return [_layer_block((len(POOL_WINDOWS), POOL_GROUP, POOL_GROUP), layer), _layer_block((1, D_POOL), layer),
            _layer_block((CONV_WIDTH, D_RNN), layer), _layer_block((1, D_RNN), layer),
            _layer_block((n_blk, RNN_BLOCK, RNN_BLOCK), layer), _layer_block((1, D_RNN), layer),
            _layer_block((n_blk, RNN_BLOCK, RNN_BLOCK), layer), _layer_block((1, D_RNN), layer),
            _layer_block((1, D_RNN), layer)]


def _mix_prompt(rest, mix_w, layer, batch, seq, h_buf):
    tt = MIX_TILE
    depth = mix_w[0].shape[0]
    steps = seq // tt
    alias_in = [] if h_buf is None else [h_buf]
    rest_specs = [pl.BlockSpec((tt, D_REST), functools.partial(lambda b, i: (b * steps + i, 0), b))
                  for b in range(batch)]
    return pl.pallas_call(
        functools.partial(_mix_prompt_kernel, batch, len(alias_in)),
        grid=(steps,),
        in_specs=rest_specs + _mix_weight_specs(layer) + [pl.BlockSpec(memory_space=pl.ANY)] * len(alias_in),
        out_specs=[pl.BlockSpec((batch, tt, D_POOL + D_RNN), lambda i: (0, i, 0)),
                   pl.BlockSpec((1, batch, D_RNN), lambda i: (layer, 0, 0))],
        out_shape=[jax.ShapeDtypeStruct((batch, seq, D_POOL + D_RNN), F32),
                   jax.ShapeDtypeStruct((depth, batch, D_RNN), F32)],
        scratch_shapes=[pltpu.VMEM((batch, HIST_ROWS + tt, D_POOL), F32),
                        pltpu.VMEM((batch, CONV_HIST + tt, D_RNN), F32),
                        pltpu.VMEM((batch, tt, D_RNN), F32),
                        pltpu.VMEM((batch, tt, D_RNN), F32),
                        pltpu.VMEM((batch, tt, D_RNN), F32),
                        pltpu.VMEM((SUBLANES, D_RNN), F32)],
        input_output_aliases={batch + len(mix_w) + j: 1 + j for j in range(len(alias_in))},
        compiler_params=_params(("arbitrary",)),
        name="mix_prompt",
    )(*([rest] * batch), *mix_w, *alias_in)


def _mix_sample_kernel(pos0, n_alias, rest_ref, pool_ref, conv_ref, h0_ref,
                       wp_ref, sp_ref, cw_ref, cb_ref, wa_ref, ba_ref, wx_ref, bx_ref, lam_ref, *refs):
    o_ref, h_ref = refs[n_alias:]
    steps = rest_ref.shape[0]
    sp_lam = _softplus(-lam_ref[...])
    pool_rows = [pool_ref[j] for j in range(POOL_BUF)] + [rest_ref[i, :, 0:D_POOL] for i in range(steps)]
    conv_rows = ([conv_ref[j] for j in range(CONV_WIDTH - 1)]
                 + [rest_ref[i, :, D_POOL:D_POOL + D_RNN] for i in range(steps)])
    h = h0_ref[...]
    for i in range(steps):
        u = pool_rows[POOL_BUF + i]
        pooled = []
        for g, win in enumerate(POOL_WINDOWS):
            sl = slice(g * POOL_GROUP, (g + 1) * POOL_GROUP)
            s = u[:, sl]
            for j in range(1, win):
                s = s + pool_rows[POOL_BUF + i - j][:, sl]
            pooled.append(s / float(min(win, pos0 + i + 1)) - u[:, sl])
        o_ref[i, :, 0:D_POOL] = _pool_project(jnp.concatenate(pooled, axis=1), wp_ref, sp_ref[...])

        xc = cb_ref[...]
        for j in range(CONV_WIDTH):
            xc = xc + conv_rows[i + j] * cw_ref[j:j + 1, :]
        a, inp = _rglru_inputs(xc, wa_ref, ba_ref[...], wx_ref, bx_ref[...], sp_lam)
        h = a * h + inp
        o_ref[i, :, D_POOL:D_POOL + D_RNN] = h * _gelu_tanh(rest_ref[i, :, D_POOL + D_RNN:D_REST])
    h_ref[0] = h


def _mix_sample(rest_t, pool_t, conv_t, state_h, mix_w, layer, pos0, h_buf):
    steps, dec_b, _ = rest_t.shape
    depth = state_h.shape[0]
    whole = lambda shape: pl.BlockSpec(shape, lambda i: (0,) * len(shape))
    alias_in = [] if h_buf is None else [h_buf]
    n_in = 4 + len(mix_w)
    return pl.pallas_call(
        functools.partial(_mix_sample_kernel, pos0, len(alias_in)),
        grid=(1,),
        in_specs=[whole(rest_t.shape), whole(pool_t.shape), whole(conv_t.shape),
                  pl.BlockSpec((None, dec_b, D_RNN), lambda i: (layer, 0, 0))]
        + _mix_weight_specs(layer) + [pl.BlockSpec(memory_space=pl.ANY)] * len(alias_in),
        out_specs=[whole((steps, dec_b, D_POOL + D_RNN)),
                   pl.BlockSpec((1, dec_b, D_RNN), lambda i: (layer, 0, 0))],
        out_shape=[jax.ShapeDtypeStruct((steps, dec_b, D_POOL + D_RNN), F32),
                   jax.ShapeDtypeStruct((depth, dec_b, D_RNN), F32)],
        input_output_aliases={n_in + j: 1 + j for j in range(len(alias_in))},
        compiler_params=_params(("arbitrary",)),
        name="mix_sample",
    )(rest_t, pool_t, conv_t, state_h, *mix_w, *alias_in)


def _out_proj_kernel(att_ref, pr_ref, x_ref, gg_ref, w_ref, gp_ref, o_ref):
    pr = pr_ref[...]
    mixed = jnp.concatenate(
        [_normalize(att_ref[...]), _normalize(pr[:, 0:D_POOL]), _normalize(pr[:, D_POOL:])],
        axis=1) * gg_ref[...]
    y = _dot(mixed.astype(BF16), w_ref[...])
    o_ref[...] = x_ref[...] + _rms(y, gp_ref[...])


def _out_proj(att, pr, x, g_grp, w_bf, g_post, layer):
    n = x.shape[0]
    return pl.pallas_call(
        _out_proj_kernel,
        grid=(n // ROW_TILE,),
        in_specs=[_row_block(D_ATT), _row_block(D_POOL + D_RNN), _row_block(D_MODEL),
                  _layer_block((1, D_MODEL), layer), _layer_block((D_MODEL, D_MODEL), layer),
                  _layer_block((1, D_MODEL), layer)],
        out_specs=_row_block(D_MODEL),
        out_shape=jax.ShapeDtypeStruct((n, D_MODEL), F32),
        compiler_params=_params(("arbitrary",)),
        name="out_proj",
    )(att, pr, x, g_grp, w_bf, g_post)


def _ffn_kernel(x_ref, g1_ref, wu_ref, wd_ref, g2_ref, o_ref, xn_scr, acc_scr):
    f = pl.program_id(1)

    @pl.when(f == 0)
    def _():
        xn_scr[...] = _rms(x_ref[...], g1_ref[...]).astype(BF16)
        acc_scr[...] = jnp.zeros_like(acc_scr)

    h = jnp.square(jnp.maximum(_dot(xn_scr[...], wu_ref[...]), 0.0))
    acc_scr[...] += _dot(h.astype(BF16), wd_ref[...])

    @pl.when(f == pl.num_programs(1) - 1)
    def _():
        o_ref[...] = x_ref[...] + _rms(acc_scr[...], g2_ref[...])


def _ffn(x, g_pre, wu_bf, wd_bf, g_post, layer):
    n = x.shape[0]
    row = pl.BlockSpec((ROW_TILE, D_MODEL), lambda i, f: (i, 0))
    return pl.pallas_call(
        _ffn_kernel,
        grid=(n // ROW_TILE, D_FF // FF_TILE),
        in_specs=[row, _layer_block((1, D_MODEL), layer),
                  pl.BlockSpec((D_MODEL, FF_TILE), lambda i, f: (0, f)),
                  pl.BlockSpec((FF_TILE, D_MODEL), lambda i, f: (f, 0)),
                  _layer_block((1, D_MODEL), layer)],
        out_specs=row,
        out_shape=jax.ShapeDtypeStruct((n, D_MODEL), F32),
        scratch_shapes=[pltpu.VMEM((ROW_TILE, D_MODEL), BF16), pltpu.VMEM((ROW_TILE, D_MODEL), F32)],
        compiler_params=_params(("arbitrary", "arbitrary")),
        name="ffn",
    )(x, g_pre, wu_bf, wd_bf, g_post)


def _ple_kernel(x_ref, p_ref, wp_ref, g_ref, wg_ref, o_ref):
    x = x_ref[...]
    e = _rms(_dot(p_ref[...].astype(BF16), wp_ref[...]), g_ref[...])
    o_ref[...] = x + e * jax.nn.sigmoid(_dot(x.astype(BF16), wg_ref[...]))


def _ple(x, p, wp_bf, g, wg_bf, layer):
    n = x.shape[0]
    return pl.pallas_call(
        _ple_kernel,
        grid=(n // ROW_TILE,),
        in_specs=[_row_block(D_MODEL), pl.BlockSpec((None, ROW_TILE, D_PLE), lambda i: (layer, i, 0)),
                  _layer_block((D_PLE, D_MODEL), layer), _layer_block((1, D_MODEL), layer),
                  _layer_block((D_MODEL, D_MODEL), layer)],
        out_specs=_row_block(D_MODEL),
        out_shape=jax.ShapeDtypeStruct((n, D_MODEL), F32),
        compiler_params=_params(("arbitrary",)),
        name="ple",
    )(x, p, wp_bf, g, wg_bf)


def kernel(x_prompt, x_sample, p_prompt, p_sample, cache_k, cache_v, page_table, state_pool, state_conv, state_h, g_pre_mix, w_in, b_sb, w_pool, s_pool, conv_w, conv_b, w_a, b_a, w_x, b_x, lam, g_grp, w_out, g_post_mix, g_pre_ffn, w_up, w_down, g_post_ffn, w_ple, g_ple, w_ple_gate):
    depth = w_in.shape[0]
    batch, seq, _ = x_prompt.shape
    dec_b, dec_t, _ = x_sample.shape
    n_p, n_s = batch * seq, dec_b * dec_t
    past_len = page_table.shape[1] * PAGE_SIZE
    assert dec_t == DEC_SEQ and dec_b % 2 == 0 and n_p % ROW_TILE == 0 and n_s % ROW_TILE == 0
    assert seq % MIX_TILE == 0 and seq % ATT_TILE == 0 and past_len + dec_t > max(POOL_WINDOWS)

    bf = lambda a: a.astype(BF16)
    vec = lambda a: a.reshape(depth, 1, -1)
    w_in_bf, w_out_bf, w_ple_bf, w_gate_bf = bf(w_in), bf(w_out), bf(w_ple), bf(w_ple_gate)
    mix_w = (bf(w_pool), vec(s_pool), conv_w, vec(conv_b), bf(w_a), vec(b_a), bf(w_x), vec(b_x), vec(lam))
    g_pre_mix, g_grp, g_post_mix, g_pre_ffn, g_post_ffn, g_ple = map(
        vec, (g_pre_mix, g_grp, g_post_mix, g_pre_ffn, g_post_ffn, g_ple))
    pp = p_prompt.reshape(depth, n_p, D_PLE)
    ps = p_sample.reshape(depth, n_s, D_PLE)
    pool_t = state_pool.transpose(0, 2, 1, 3)
    conv_t = state_conv.transpose(0, 2, 1, 3)

    xp = x_prompt.reshape(n_p, D_MODEL)
    xs = x_sample.reshape(n_s, D_MODEL)
    kv_p = kv_s = hp_buf = hs_buf = None
    pool_p, conv_p, pool_s, conv_s = [], [], [], []

    def dense_tail(att, pr, x, p, wu_bf, wd_bf, l):
        x = _out_proj(att, pr, x, g_grp, w_out_bf, g_post_mix, l)
        x = _ffn(x, g_pre_ffn, wu_bf, wd_bf, g_post_ffn, l)
        return _ple(x, p, w_ple_bf, g_ple, w_gate_bf, l)

    for l in range(depth):
        q, k, v, rest, *kv_p = _in_proj(xp, g_pre_mix, w_in_bf, l, kv_p)
        att, wu_bf, wd_bf = _attn_prompt(q, k, v, b_sb[l], w_up, w_down, l, batch, seq)
        pr, hp_buf = _mix_prompt(rest, mix_w, l, batch, seq, hp_buf)
        xp = dense_tail(att, pr.reshape(n_p, -1), xp, pp, wu_bf, wd_bf, l)
        tail = lambda rows, lo: jnp.stack(
            [rest[(b + 1) * seq - rows:(b + 1) * seq, lo:lo + D_POOL] for b in range(batch)])
        pool_p.append(tail(POOL_BUF, 0))
        conv_p.append(tail(CONV_WIDTH - 1, D_POOL))

        q, _, _, rest, *kv_s = _in_proj(xs, g_pre_mix, w_in_bf, l, kv_s)
        bias_rows = jnp.repeat(b_sb[l], PAIR_ROWS).reshape(PAIR_ROWS * N_HEADS, 1)
        att = _attn_sample(q.astype(F32), kv_s[0], kv_s[1], l, bias_rows, cache_k, cache_v, page_table)
        rest = rest.reshape(dec_b, dec_t, D_REST)
        pr, hs_buf = _mix_sample(rest.transpose(1, 0, 2), pool_t[l], conv_t[l], state_h,
                                 mix_w, l, past_len, hs_buf)
        xs = dense_tail(att, pr.transpose(1, 0, 2).reshape(n_s, -1), xs, ps, wu_bf, wd_bf, l)
        pool_s.append(jnp.concatenate([state_pool[l], rest[:, :, 0:D_POOL]], axis=1)[:, -POOL_BUF:])
        conv_s.append(jnp.concatenate([state_conv[l], rest[:, :, D_POOL:D_POOL + D_RNN]],
                                      axis=1)[:, -(CONV_WIDTH - 1):])

    kp, vp = kv_p
    ks, vs = kv_s
    heads = lambda a, nb: a.reshape(depth, nb, -1, N_HEADS, HEAD_DIM)
    return (xp.reshape(batch, seq, D_MODEL), xs.reshape(dec_b, dec_t, D_MODEL),
            heads(kp, batch), heads(vp, batch), jnp.stack(pool_p), jnp.stack(conv_p), hp_buf,
            heads(ks, dec_b), heads(vs, dec_b), jnp.stack(pool_s), jnp.stack(conv_s), hs_buf)
```

```python
import functools

import jax
import jax.numpy as jnp
from jax import lax
from jax.experimental import pallas as pl
from jax.experimental.pallas import tpu as pltpu

F32 = jnp.float32
BF16 = jnp.bfloat16

D_MODEL = 2048
D_ATT = 1024
HEAD_DIM = 128
N_HEADS = 8
D_POOL = 512
POOL_WINDOWS = (2, 4, 8, 16)
POOL_GROUP = 128
POOL_BUF = 15
D_RNN = 512
RNN_BLOCK = 128
CONV_WIDTH = 4
RG_C = 8.0
D_FF = 8192
D_PLE = 256
PAGE_SIZE = 128
DEC_SEQ = 4
EPS = 1e-6
D_REST = D_POOL + 2 * D_RNN
D_IN = 3 * D_ATT + D_REST
SCALE = HEAD_DIM ** -0.5

VMEM_LIMIT_V7X = 56 * 1024 * 1024
MXU_DIM_V7X = 256
SUBLANES = 8
ROW_TILE = 512
FF_TILE = 1024
FUSED_FF_TILE = 512
ATT_TILE = 512
MIX_TILE = 256
HIST_ROWS = 16
CONV_HIST = 8
PAIR_ROWS = 2 * DEC_SEQ


def _params(sem, vmem=VMEM_LIMIT_V7X):
    return pltpu.CompilerParams(dimension_semantics=sem, vmem_limit_bytes=vmem)


def _layer_block(shape, layer):
    return pl.BlockSpec((None,) + tuple(shape), lambda *_: (layer,) + (0,) * len(shape),
                        pipeline_mode=pl.Buffered(1))


def _whole_block(shape):
    return pl.BlockSpec(tuple(shape), lambda *_: (0,) * len(shape), pipeline_mode=pl.Buffered(1))


def _row_block(width):
    return pl.BlockSpec((ROW_TILE, width), lambda i: (i, 0))


def _rms(x, g):
    return x * lax.rsqrt(jnp.mean(x * x, axis=-1, keepdims=True) + EPS) * g


def _normalize(x):
    return x * lax.rsqrt(jnp.mean(x * x, axis=-1, keepdims=True) + EPS)


def _dot(a, b):
    return jnp.dot(a, b, preferred_element_type=F32)


def _dot_nt(a, b):
    return lax.dot_general(a, b, (((1,), (1,)), ((), ())), preferred_element_type=F32)


def _softplus(x):
    return jnp.maximum(x, 0.0) + jnp.log1p(jnp.exp(-jnp.abs(x)))


def _gelu_tanh(x):
    c = 0.7978845608028654
    return 0.5 * x * (1.0 + jnp.tanh(c * (x + 0.044715 * (x * x * x))))


def _in_proj_kernel(n_alias, x_ref, g_ref, w_ref, *refs):
    q_ref, k_ref, v_ref, r_ref, kh_ref, vh_ref = refs[n_alias:]
    xn = _rms(x_ref[...], g_ref[...]).astype(BF16)
    q_ref[...] = _dot(xn, w_ref[:, 0:D_ATT]).astype(BF16)
    k = _dot(xn, w_ref[:, D_ATT:2 * D_ATT])
    v = _dot(xn, w_ref[:, 2 * D_ATT:3 * D_ATT])
    k_ref[...] = k.astype(BF16)
    v_ref[...] = v.astype(BF16)
    r_ref[...] = _dot(xn, w_ref[:, 3 * D_ATT:D_IN])
    for h in range(N_HEADS):
        kh_ref[pl.ds(h, ROW_TILE, stride=N_HEADS), :] = k[:, h * HEAD_DIM:(h + 1) * HEAD_DIM]
        vh_ref[pl.ds(h, ROW_TILE, stride=N_HEADS), :] = v[:, h * HEAD_DIM:(h + 1) * HEAD_DIM]


def _in_proj(x, g, w_bf, layer, kv_bufs):
    depth = g.shape[0]
    n = x.shape[0]
    heads = pl.BlockSpec((None, ROW_TILE * N_HEADS, HEAD_DIM), lambda i: (layer, i, 0))
    stack = jax.ShapeDtypeStruct((depth, n * N_HEADS, HEAD_DIM), F32)
    alias_in = [] if kv_bufs is None else list(kv_bufs)
    n_fixed = 3
    return pl.pallas_call(
        functools.partial(_in_proj_kernel, len(alias_in)),
        grid=(n // ROW_TILE,),
        in_specs=[_row_block(D_MODEL), _layer_block((1, D_MODEL), layer), _whole_block((D_MODEL, D_IN))]
        + [pl.BlockSpec(memory_space=pl.ANY)] * len(alias_in),
        out_specs=[_row_block(D_ATT), _row_block(D_ATT), _row_block(D_ATT), _row_block(D_REST), heads, heads],
        out_shape=[jax.ShapeDtypeStruct((n, D_ATT), BF16)] * 3 + [jax.ShapeDtypeStruct((n, D_REST), F32)]
        + [stack, stack],
        input_output_aliases={n_fixed + j: 4 + j for j in range(len(alias_in))},
        compiler_params=_params(("arbitrary",)),
        name="in_proj",
    )(x, g, w_bf, *alias_in)


def _strict_upper_ones(n):
    j = lax.broadcasted_iota(jnp.int32, (n, n), 0)
    s = lax.broadcasted_iota(jnp.int32, (n, n), 1)
    return jnp.where(j > s, 1.0, 0.0).astype(BF16)


def _sb_logs(z, mask):
    log_beta = jnp.minimum(z, 0.0) - jnp.log(1.0 + jnp.exp(-jnp.abs(z)))
    log_stay = log_beta - z
    if mask is not None:
        log_stay = jnp.where(mask, log_stay, 0.0)
    return log_beta, log_stay


def _sb_weights(log_beta, log_stay, tri, carry, mask):
    n = tri.shape[0]
    chunks = log_stay.shape[1] // n
    hi = log_stay.astype(BF16)
    lo = (log_stay - hi.astype(F32)).astype(BF16)
    tri2 = jnp.concatenate([tri, tri], axis=0)
    inner, sums = [], []
    for c in range(chunks):
        sl = slice(c * n, (c + 1) * n)
        inner.append(_dot(jnp.concatenate([hi[:, sl], lo[:, sl]], axis=1), tri2))
        sums.append(jnp.sum(log_stay[:, sl], axis=-1, keepdims=True))
    later = [None] * chunks
    for c in reversed(range(chunks)):
        later[c] = inner[c] + carry
        carry = carry + sums[c]
    w = jnp.exp(log_beta + jnp.concatenate(later, axis=1))
    if mask is not None:
        w = jnp.where(mask, w, 0.0)
    return w, carry


def _attn_prompt_kernel(n_casts, bias_ref, q_ref, k_ref, v_ref, *refs):
    o_ref = refs[n_casts]
    for w_ref, w_bf_ref in zip(refs[:n_casts], refs[n_casts + 1:], strict=True):
        w_bf_ref[...] = w_ref[...].astype(BF16)
    t = ATT_TILE
    n_blk = q_ref.shape[0] // t
    bias = bias_ref[pl.program_id(1)]
    tri = _strict_upper_ones(MXU_DIM_V7X)

    def tile(q_bf, k0, n_keys, carry, mask):
        z = _dot_nt(q_bf, k_ref[pl.ds(k0, n_keys), :]) * SCALE + bias
        log_beta, log_stay = _sb_logs(z, mask)
        w, carry = _sb_weights(log_beta, log_stay, tri, carry, mask)
        return _dot(w.astype(BF16), v_ref[pl.ds(k0, n_keys), :]), carry

    def causal(n_rows, n_keys, row0):
        row = lax.broadcasted_iota(jnp.int32, (n_rows, n_keys), 0)
        col = lax.broadcasted_iota(jnp.int32, (n_rows, n_keys), 1)
        return col < row + row0

    diag_mask = causal(t, t, 0)

    def q_block(qi, _):
        q0 = pl.multiple_of(qi * t, t)
        q_bf = q_ref[pl.ds(q0, t), :]
        acc, carry = tile(q_bf, q0, t, jnp.zeros((t, 1), F32), diag_mask)

        def k_block(it, state):
            acc, carry = state
            out, carry = tile(q_bf, pl.multiple_of((qi - 1 - it) * t, t), t, carry, None)
            return acc + out, carry

        acc, _ = lax.fori_loop(0, qi, k_block, (acc, carry))
        o_ref[pl.ds(q0, t), :] = acc
        return 0

    lax.fori_loop(0, n_blk, q_block, 0)


def _attn_prompt(q, k, v, b_sb, casts, batch, seq):
    blk = pl.BlockSpec((seq, HEAD_DIM), lambda b, h: (b, h))
    steps = batch * N_HEADS
    cast_in, cast_out, cast_shape = [], [], []
    for w, layer in casts:
        _, rows, cols = w.shape
        assert rows % (16 * steps) == 0
        cast_in.append(pl.BlockSpec((None, rows // steps, cols),
                                    functools.partial(lambda layer, b, h: (layer, b * N_HEADS + h, 0), layer)))
        cast_out.append(pl.BlockSpec((rows // steps, cols), lambda b, h: (b * N_HEADS + h, 0)))
        cast_shape.append(jax.ShapeDtypeStruct((rows, cols), BF16))
    return pl.pallas_call(
        functools.partial(_attn_prompt_kernel, len(casts)),
        grid=(batch, N_HEADS),
        in_specs=[pl.BlockSpec(memory_space=pltpu.SMEM), blk, blk, blk] + cast_in,
        out_specs=[blk] + cast_out,
        out_shape=[jax.ShapeDtypeStruct((batch * seq, D_ATT), F32)] + cast_shape,
        compiler_params=_params(("arbitrary", "arbitrary")),
        name="attn_prompt",
    )(b_sb, q, k, v, *[w for w, _ in casts])


def _decode_half(unit, bias_ref, q_ref, kn_ref, vn_ref, k_refs, v_refs, o_ref, kpad_ref, vpad_ref, acc_ref, carry_ref):
    n_half = len(k_refs)
    seq, first = unit // 2, unit % 2 == 0
    rows = PAIR_ROWS * N_HEADS
    t = PAGE_SIZE

    def keys_by_heads(ref, n_keys):
        return jnp.concatenate([ref[pl.ds(h, n_keys, stride=N_HEADS), :] for h in range(N_HEADS)], axis=1)

    kpad_ref[0:PAIR_ROWS, :] = keys_by_heads(kn_ref, PAIR_ROWS)
    vpad_ref[0:PAIR_ROWS, :] = keys_by_heads(vn_ref, PAIR_ROWS)

    q_rep = jnp.concatenate([q_ref[0]] * N_HEADS, axis=0)
    r_id = lax.broadcasted_iota(jnp.int32, (rows, D_ATT), 0)
    c_id = lax.broadcasted_iota(jnp.int32, (rows, D_ATT), 1)
    q_bd = jnp.where(c_id // HEAD_DIM == r_id // PAIR_ROWS, q_rep, 0.0).astype(BF16)

    z = [_dot_nt(q_bd, keys_by_heads(k_refs[p], t).astype(BF16)) for p in range(n_half)]
    z.append(_dot_nt(q_bd, kpad_ref[...].astype(BF16)))
    z = jnp.concatenate(z, axis=1) * SCALE + bias_ref[...]
    row = lax.broadcasted_iota(jnp.int32, z.shape, 0) % PAIR_ROWS
    col = lax.broadcasted_iota(jnp.int32, z.shape, 1) - n_half * t
    n_new = jnp.where(first, PAIR_ROWS, 0)
    mask = (col < 0) | ((col < n_new) & (col // DEC_SEQ == row // DEC_SEQ) & (col % DEC_SEQ < row % DEC_SEQ))
    log_beta, log_stay = _sb_logs(z, mask)
    carry = jnp.where(first, 0.0, carry_ref[:, 0:1])
    w, carry = _sb_weights(log_beta, log_stay, _strict_upper_ones(t), carry, mask)
    w = w.astype(BF16)
    acc = jnp.where(first, 0.0, acc_ref[...]) + _dot(w[:, n_half * t:], vpad_ref[...].astype(BF16))
    for p in range(n_half):
        acc = acc + _dot(w[:, p * t:(p + 1) * t], keys_by_heads(v_refs[p], t).astype(BF16))
    acc_ref[...] = acc
    carry_ref[...] = jnp.broadcast_to(carry, carry_ref.shape)

    o8 = jnp.concatenate(
        [acc[h * PAIR_ROWS:(h + 1) * PAIR_ROWS, h * HEAD_DIM:(h + 1) * HEAD_DIM] for h in range(N_HEADS)], axis=1)
    last = jnp.logical_not(first)

    def write():
        @pl.when(last & (seq % 2 == 0))
        def _():
            o_ref[0, 0:DEC_SEQ, :] = o8[0:DEC_SEQ]

        @pl.when(last & (seq % 2 == 1))
        def _():
            o_ref[0, DEC_SEQ:PAIR_ROWS, :] = o8[DEC_SEQ:PAIR_ROWS]

    return write


def _decode_init(kpad_ref, vpad_ref, acc_ref, carry_ref):
    kpad_ref[...] = jnp.zeros_like(kpad_ref)
    vpad_ref[...] = jnp.zeros_like(vpad_ref)
    acc_ref[...] = jnp.zeros_like(acc_ref)
    carry_ref[...] = jnp.zeros_like(carry_ref)


def _pool_project(pooled, wp_ref, sp):
    outs = []
    for g in range(len(POOL_WINDOWS)):
        sl = slice(g * POOL_GROUP, (g + 1) * POOL_GROUP)
        outs.append(_dot(pooled[:, sl].astype(BF16), wp_ref[g]))
    return jnp.concatenate(outs, axis=1) * sp


def _rglru_inputs(xc, wa_ref, ba, wx_ref, bx, sp_lam):
    ra, ia = [], []
    for g in range(D_RNN // RNN_BLOCK):
        xg = xc[:, g * RNN_BLOCK:(g + 1) * RNN_BLOCK].astype(BF16)
        ra.append(_dot(xg, wa_ref[g]))
        ia.append(_dot(xg, wx_ref[g]))
    r = jax.nn.sigmoid(jnp.concatenate(ra, axis=1) + ba)
    i = jax.nn.sigmoid(jnp.concatenate(ia, axis=1) + bx)
    log_a = -RG_C * r * sp_lam
    a = jnp.exp(log_a)
    inp = jnp.sqrt(-jnp.tanh(log_a) * (a * a + 1.0)) * (i * xc)
    return a, inp


def _mix_prompt_kernel(nb, n_alias, *refs):
    rest_refs = refs[:nb]
    wp_ref, sp_ref, cw_ref, cb_ref, wa_ref, ba_ref, wx_ref, bx_ref, lam_ref = refs[nb:nb + 9]
    o_ref, h_ref, pool_ext, conv_ext, a_scr, b_scr, hs_scr, h_scr = refs[nb + 9 + n_alias:]
    tt = MIX_TILE
    step = pl.program_id(0)

    @pl.when(step == 0)
    def _():
        pool_ext[:, 0:HIST_ROWS, :] = jnp.zeros((nb, HIST_ROWS, D_POOL), F32)
        conv_ext[:, 0:CONV_HIST, :] = jnp.zeros((nb, CONV_HIST, D_RNN), F32)
        h_scr[...] = jnp.zeros_like(h_scr)

    pos = step * tt + lax.broadcasted_iota(jnp.int32, (tt, POOL_GROUP), 0)
    sp_lam = _softplus(-lam_ref[...])
    for b in range(nb):
        rest_ref = rest_refs[b]
        u = rest_ref[:, 0:D_POOL]
        pool_ext[b, HIST_ROWS:HIST_ROWS + tt, :] = u
        pooled = []
        for g, win in enumerate(POOL_WINDOWS):
            sl = slice(g * POOL_GROUP, (g + 1) * POOL_GROUP)
            s = u[:, sl]
            for j in range(1, win):
                s = s + pool_ext[b, HIST_ROWS - j:HIST_ROWS - j + tt, sl]
            cnt = jnp.minimum(win, pos + 1).astype(F32)
            pooled.append(s / cnt - u[:, sl])
        o_ref[b, :, 0:D_POOL] = _pool_project(jnp.concatenate(pooled, axis=1), wp_ref, sp_ref[...])
        pool_ext[b, 0:HIST_ROWS, :] = pool_ext[b, tt:tt + HIST_ROWS, :]

        ur = rest_ref[:, D_POOL:D_POOL + D_RNN]
        conv_ext[b, CONV_HIST:CONV_HIST + tt, :] = ur
        xc = cb_ref[...]
        for j in range(CONV_WIDTH):
            off = CONV_HIST - (CONV_WIDTH - 1) + j
            xc = xc + conv_ext[b, off:off + tt, :] * cw_ref[j:j + 1, :]
        conv_ext[b, 0:CONV_HIST, :] = conv_ext[b, tt:tt + CONV_HIST, :]
        a, inp = _rglru_inputs(xc, wa_ref, ba_ref[...], wx_ref, bx_ref[...], sp_lam)
        a_scr[b] = a
        b_scr[b] = inp

    def scan_row(t, hs):
        new = []
        for b in range(nb):
            h = a_scr[b, pl.ds(t, 1), :] * hs[b] + b_scr[b, pl.ds(t, 1), :]
            hs_scr[b, pl.ds(t, 1), :] = h
            new.append(h)
        return tuple(new)

    hs = lax.fori_loop(0, tt, scan_row, tuple(h_scr[b:b + 1, :] for b in range(nb)), unroll=8)
    for b in range(nb):
        h_scr[b:b + 1, :] = hs[b]
        o_ref[b, :, D_POOL:D_POOL + D_RNN] = hs_scr[b] * _gelu_tanh(rest_refs[b][:, D_POOL + D_RNN:D_REST])
    h_ref[0] = h_scr[0:nb, :]


def _mix_weight_specs(layer):
    n_blk = D_RNN // RNN_BLOCK
    return [_layer_block((len(POOL_WINDOWS), POOL_GROUP, POOL_GROUP), layer), _layer_block((1, D_POOL), layer),
            _layer_block((CONV_WIDTH, D_RNN), layer), _layer_block((1, D_RNN), layer),
            _layer_block((n_blk, RNN_BLOCK, RNN_BLOCK), layer), _layer_block((1, D_RNN), layer),
            _layer_block((n_blk, RNN_BLOCK, RNN_BLOCK), layer), _layer_block((1, D_RNN), layer),
            _layer_block((1, D_RNN), layer)]


def _mix_prompt(rest, mix_w, layer, batch, seq, h_buf):
    tt = MIX_TILE
    depth = mix_w[0].shape[0]
    steps = seq // tt
    alias_in = [] if h_buf is None else [h_buf]
    rest_specs = [pl.BlockSpec((tt, D_REST), functools.partial(lambda b, i: (b * steps + i, 0), b))
                  for b in range(batch)]
    return pl.pallas_call(
        functools.partial(_mix_prompt_kernel, batch, len(alias_in)),
        grid=(steps,),
        in_specs=rest_specs + _mix_weight_specs(layer) + [pl.BlockSpec(memory_space=pl.ANY)] * len(alias_in),
        out_specs=[pl.BlockSpec((batch, tt, D_POOL + D_RNN), lambda i: (0, i, 0)),
                   pl.BlockSpec((1, batch, D_RNN), lambda i: (layer, 0, 0))],
        out_shape=[jax.ShapeDtypeStruct((batch, seq, D_POOL + D_RNN), F32),
                   jax.ShapeDtypeStruct((depth, batch, D_RNN), F32)],
        scratch_shapes=[pltpu.VMEM((batch, HIST_ROWS + tt, D_POOL), F32),
                        pltpu.VMEM((batch, CONV_HIST + tt, D_RNN), F32),
                        pltpu.VMEM((batch, tt, D_RNN), F32),
                        pltpu.VMEM((batch, tt, D_RNN), F32),
                        pltpu.VMEM((batch, tt, D_RNN), F32),
                        pltpu.VMEM((SUBLANES, D_RNN), F32)],
        input_output_aliases={batch + len(mix_w) + j: 1 + j for j in range(len(alias_in))},
        compiler_params=_params(("arbitrary",)),
        name="mix_prompt",
    )(*([rest] * batch), *mix_w, *alias_in)


def _mix_sample_kernel(pos0, n_alias, rest_ref, pool_ref, conv_ref, h0_ref,
                       wp_ref, sp_ref, cw_ref, cb_ref, wa_ref, ba_ref, wx_ref, bx_ref, lam_ref, *refs):
    o_ref, h_ref = refs[n_alias:]
    steps = rest_ref.shape[0]
    sp_lam = _softplus(-lam_ref[...])
    pool_rows = [pool_ref[j] for j in range(POOL_BUF)] + [rest_ref[i, :, 0:D_POOL] for i in range(steps)]
    conv_rows = ([conv_ref[j] for j in range(CONV_WIDTH - 1)]
                 + [rest_ref[i, :, D_POOL:D_POOL + D_RNN] for i in range(steps)])
    h = h0_ref[...]
    for i in range(steps):
        u = pool_rows[POOL_BUF + i]
        pooled = []
        for g, win in enumerate(POOL_WINDOWS):
            sl = slice(g * POOL_GROUP, (g + 1) * POOL_GROUP)
            s = u[:, sl]
            for j in range(1, win):
                s = s + pool_rows[POOL_BUF + i - j][:, sl]
            pooled.append(s / float(min(win, pos0 + i + 1)) - u[:, sl])
        o_ref[i, :, 0:D_POOL] = _pool_project(jnp.concatenate(pooled, axis=1), wp_ref, sp_ref[...])

        xc = cb_ref[...]
        for j in range(CONV_WIDTH):
            xc = xc + conv_rows[i + j] * cw_ref[j:j + 1, :]
        a, inp = _rglru_inputs(xc, wa_ref, ba_ref[...], wx_ref, bx_ref[...], sp_lam)
        h = a * h + inp
        o_ref[i, :, D_POOL:D_POOL + D_RNN] = h * _gelu_tanh(rest_ref[i, :, D_POOL + D_RNN:D_REST])
    h_ref[0] = h


def _mix_sample(rest_t, pool_t, conv_t, state_h, mix_w, layer, pos0, h_buf):
    steps, dec_b, _ = rest_t.shape
    depth = state_h.shape[0]
    whole = lambda shape: pl.BlockSpec(shape, lambda i: (0,) * len(shape))
    alias_in = [] if h_buf is None else [h_buf]
    n_in = 4 + len(mix_w)
    return pl.pallas_call(
        functools.partial(_mix_sample_kernel, pos0, len(alias_in)),
        grid=(1,),
        in_specs=[whole(rest_t.shape), whole(pool_t.shape), whole(conv_t.shape),
                  pl.BlockSpec((None, dec_b, D_RNN), lambda i: (layer, 0, 0))]
        + _mix_weight_specs(layer) + [pl.BlockSpec(memory_space=pl.ANY)] * len(alias_in),
        out_specs=[whole((steps, dec_b, D_POOL + D_RNN)),
                   pl.BlockSpec((1, dec_b, D_RNN), lambda i: (layer, 0, 0))],
        out_shape=[jax.ShapeDtypeStruct((steps, dec_b, D_POOL + D_RNN), F32),
                   jax.ShapeDtypeStruct((depth, dec_b, D_RNN), F32)],
        input_output_aliases={n_in + j: 1 + j for j in range(len(alias_in))},
        compiler_params=_params(("arbitrary",)),
        name="mix_sample",
    )(rest_t, pool_t, conv_t, state_h, *mix_w, *alias_in)


def _out_proj_kernel(att_ref, pr_ref, x_ref, gg_ref, w_ref, gp_ref, o_ref):
    pr = pr_ref[...]
    mixed = jnp.concatenate(
        [_normalize(att_ref[...]), _normalize(pr[:, 0:D_POOL]), _normalize(pr[:, D_POOL:])],
        axis=1) * gg_ref[...]
    y = _dot(mixed.astype(BF16), w_ref[...])
    o_ref[...] = x_ref[...] + _rms(y, gp_ref[...])


def _out_proj(att, pr, x, g_grp, w_bf, g_post, layer):
    n = x.shape[0]
    return pl.pallas_call(
        _out_proj_kernel,
        grid=(n // ROW_TILE,),
        in_specs=[_row_block(D_ATT), _row_block(D_POOL + D_RNN), _row_block(D_MODEL),
                  _layer_block((1, D_MODEL), layer), _whole_block((D_MODEL, D_MODEL)),
                  _layer_block((1, D_MODEL), layer)],
        out_specs=_row_block(D_MODEL),
        out_shape=jax.ShapeDtypeStruct((n, D_MODEL), F32),
        compiler_params=_params(("arbitrary",)),
        name="out_proj",
    )(att, pr, x, g_grp, w_bf, g_post)


def _ffn_kernel(x_ref, g1_ref, wu_ref, wd_ref, g2_ref, o_ref, xn_scr, acc_scr, side_init=None, side_work=None):
    f = pl.program_id(1)

    @pl.when(f == 0)
    def _():
        xn_scr[...] = _rms(x_ref[...], g1_ref[...]).astype(BF16)
        acc_scr[...] = jnp.zeros_like(acc_scr)
        if side_init is not None:
            side_init()

    h = jnp.square(jnp.maximum(_dot(xn_scr[...], wu_ref[...]), 0.0))
    acc_scr[...] += _dot(h.astype(BF16), wd_ref[...])
    side_done = side_work() if side_work is not None else None

    @pl.when(f == pl.num_programs(1) - 1)
    def _():
        o_ref[...] = x_ref[...] + _rms(acc_scr[...], g2_ref[...])

    if side_done is not None:
        side_done()


def _ffn(x, g_pre, wu_bf, wd_bf, g_post, layer):
    n = x.shape[0]
    row = pl.BlockSpec((ROW_TILE, D_MODEL), lambda i, f: (i, 0))
    return pl.pallas_call(
        _ffn_kernel,
        grid=(n // ROW_TILE, D_FF // FF_TILE),
        in_specs=[row, _layer_block((1, D_MODEL), layer),
                  pl.BlockSpec((D_MODEL, FF_TILE), lambda i, f: (0, f)),
                  pl.BlockSpec((FF_TILE, D_MODEL), lambda i, f: (f, 0)),
                  _layer_block((1, D_MODEL), layer)],
        out_specs=row,
        out_shape=jax.ShapeDtypeStruct((n, D_MODEL), F32),
        scratch_shapes=[pltpu.VMEM((ROW_TILE, D_MODEL), BF16), pltpu.VMEM((ROW_TILE, D_MODEL), F32)],
        compiler_params=_params(("arbitrary", "arbitrary")),
        name="ffn",
    )(x, g_pre, wu_bf, wd_bf, g_post)


def _ffn_decode_kernel(n_half, pt_ref, x_ref, g1_ref, wu_ref, wd_ref, g2_ref, bias_ref, q_ref, kn_ref, vn_ref, *rest):
    del pt_ref
    k_refs, v_refs = rest[:n_half], rest[n_half:2 * n_half]
    o_ref, od_ref, xn_scr, acc_scr, kpad_ref, vpad_ref, dacc_ref, dcarry_ref = rest[2 * n_half:]
    unit = pl.program_id(0) * pl.num_programs(1) + pl.program_id(1)

    def init():
        pl.when(unit == 0)(functools.partial(_decode_init, kpad_ref, vpad_ref, dacc_ref, dcarry_ref))

    decode = functools.partial(_decode_half, unit, bias_ref, q_ref, kn_ref, vn_ref, k_refs, v_refs, od_ref,
                               kpad_ref, vpad_ref, dacc_ref, dcarry_ref)
    _ffn_kernel(x_ref, g1_ref, wu_ref, wd_ref, g2_ref, o_ref, xn_scr, acc_scr, side_init=init, side_work=decode)


def _ffn_decode(x, g_pre, wu_bf, wd_bf, g_post, layer, q_s, k_new, v_new, bias_rows, cache_k, cache_v, page_table):
    n = x.shape[0]
    dec_b, n_pages = page_table.shape
    depth, n_pool = cache_k.shape[:2]
    n_f = D_FF // FUSED_FF_TILE
    n_half = n_pages // 2
    assert (n // ROW_TILE) * n_f == 2 * dec_b and n_pages % 2 == 0
    flat_cache = lambda c: c.reshape(depth, n_pool, PAGE_SIZE * N_HEADS, HEAD_DIM)
    unit = lambda i, f: i * n_f + f
    row = pl.BlockSpec((ROW_TILE, D_MODEL), lambda i, f, pt: (i, 0))
    pair_spec = pl.BlockSpec((1, PAIR_ROWS, D_ATT), lambda i, f, pt: (unit(i, f) // 4, 0, 0))
    new_spec = pl.BlockSpec((None, PAIR_ROWS * N_HEADS, HEAD_DIM), lambda i, f, pt: (layer, unit(i, f) // 4, 0))

    def page_spec(p):
        def index(i, f, pt):
            u = unit(i, f)
            return layer, pt[u // 2, (1 - u % 2) * n_half + p], 0, 0
        return pl.BlockSpec((None, None, PAGE_SIZE * N_HEADS, HEAD_DIM), index)

    grid_spec = pltpu.PrefetchScalarGridSpec(
        num_scalar_prefetch=1,
        grid=(n // ROW_TILE, n_f),
        in_specs=[row, _layer_block((1, D_MODEL), layer),
                  pl.BlockSpec((D_MODEL, FUSED_FF_TILE), lambda i, f, pt: (0, f)),
                  pl.BlockSpec((FUSED_FF_TILE, D_MODEL), lambda i, f, pt: (f, 0)),
                  _layer_block((1, D_MODEL), layer),
                  pl.BlockSpec((PAIR_ROWS * N_HEADS, 1), lambda i, f, pt: (0, 0)), pair_spec, new_spec, new_spec]
        + [page_spec(p) for p in range(n_half)] * 2,
        out_specs=[row, pair_spec],
        scratch_shapes=[pltpu.VMEM((ROW_TILE, D_MODEL), BF16), pltpu.VMEM((ROW_TILE, D_MODEL), F32),
                        pltpu.VMEM((PAGE_SIZE, D_ATT), F32), pltpu.VMEM((PAGE_SIZE, D_ATT), F32),
                        pltpu.VMEM((PAIR_ROWS * N_HEADS, D_ATT), F32),
                        pltpu.VMEM((PAIR_ROWS * N_HEADS, HEAD_DIM), F32)],
    )
    y, att = pl.pallas_call(
        functools.partial(_ffn_decode_kernel, n_half),
        grid_spec=grid_spec,
        out_shape=[jax.ShapeDtypeStruct((n, D_MODEL), F32),
                   jax.ShapeDtypeStruct((dec_b // 2, PAIR_ROWS, D_ATT), F32)],
        compiler_params=_params(("arbitrary", "arbitrary")),
        name="ffn_decode",
    )(page_table, x, g_pre, wu_bf, wd_bf, g_post, bias_rows, q_s.reshape(dec_b // 2, PAIR_ROWS, D_ATT),
      k_new, v_new, *([flat_cache(cache_k)] * n_half), *([flat_cache(cache_v)] * n_half))
    return y, att.reshape(dec_b * DEC_SEQ, D_ATT)


def _ple_kernel(x_ref, p_ref, wp_ref, g_ref, wg_ref, o_ref):
    x = x_ref[...]
    e = _rms(_dot(p_ref[...].astype(BF16), wp_ref[...]), g_ref[...])
    o_ref[...] = x + e * jax.nn.sigmoid(_dot(x.astype(BF16), wg_ref[...]))


def _ple(x, p, wp_bf, g, wg_bf, layer):
    n = x.shape[0]
    return pl.pallas_call(
        _ple_kernel,
        grid=(n // ROW_TILE,),
        in_specs=[_row_block(D_MODEL), pl.BlockSpec((None, ROW_TILE, D_PLE), lambda i: (layer, i, 0)),
                  _layer_block((D_PLE, D_MODEL), layer), _layer_block((1, D_MODEL), layer),
                  _whole_block((D_MODEL, D_MODEL))],
        out_specs=_row_block(D_MODEL),
        out_shape=jax.ShapeDtypeStruct((n, D_MODEL), F32),
        compiler_params=_params(("arbitrary",)),
        name="ple",
    )(x, p, wp_bf, g, wg_bf)


def kernel(x_prompt, x_sample, p_prompt, p_sample, cache_k, cache_v, page_table, state_pool, state_conv, state_h, g_pre_mix, w_in, b_sb, w_pool, s_pool, conv_w, conv_b, w_a, b_a, w_x, b_x, lam, g_grp, w_out, g_post_mix, g_pre_ffn, w_up, w_down, g_post_ffn, w_ple, g_ple, w_ple_gate):
    depth = w_in.shape[0]
    batch, seq, _ = x_prompt.shape
    dec_b, dec_t, _ = x_sample.shape
    n_p, n_s = batch * seq, dec_b * dec_t
    past_len = page_table.shape[1] * PAGE_SIZE
    assert dec_t == DEC_SEQ and dec_b % 2 == 0 and n_p % ROW_TILE == 0 and n_s % ROW_TILE == 0
    assert seq % MIX_TILE == 0 and seq % ATT_TILE == 0 and past_len + dec_t > max(POOL_WINDOWS)

    bf = lambda a: a.astype(BF16)
    vec = lambda a: a.reshape(depth, 1, -1)
    w_in_bf = bf(w_in[0])
    w_ple_bf = bf(w_ple)
    mix_w = (bf(w_pool), vec(s_pool), conv_w, vec(conv_b), bf(w_a), vec(b_a), bf(w_x), vec(b_x), vec(lam))
    g_pre_mix, g_grp, g_post_mix, g_pre_ffn, g_post_ffn, g_ple = map(
        vec, (g_pre_mix, g_grp, g_post_mix, g_pre_ffn, g_post_ffn, g_ple))
    pp = p_prompt.reshape(depth, n_p, D_PLE)
    ps = p_sample.reshape(depth, n_s, D_PLE)
    pool_t = state_pool.transpose(0, 2, 1, 3)
    conv_t = state_conv.transpose(0, 2, 1, 3)

    xp = x_prompt.reshape(n_p, D_MODEL)
    xs = x_sample.reshape(n_s, D_MODEL)
    kv_p = kv_s = hp_buf = hs_buf = None
    pool_p, conv_p, pool_s, conv_s = [], [], [], []

    for l in range(depth):
        q, k, v, rest, *kv_p = _in_proj(xp, g_pre_mix, w_in_bf, l, kv_p)
        casts = [(w_out, l), (w_up, l), (w_down, l), (w_ple_gate, l)] + ([(w_in, l + 1)] if l + 1 < depth else [])
        att, w_out_bf, wu_bf, wd_bf, w_gate_bf, *w_in_next = _attn_prompt(q, k, v, b_sb[l], casts, batch, seq)
        pr, hp_buf = _mix_prompt(rest, mix_w, l, batch, seq, hp_buf)
        xp = _out_proj(att, pr.reshape(n_p, -1), xp, g_grp, w_out_bf, g_post_mix, l)
        tail = lambda rows, lo: jnp.stack(
            [rest[(b + 1) * seq - rows:(b + 1) * seq, lo:lo + D_POOL] for b in range(batch)])
        pool_p.append(tail(POOL_BUF, 0))
        conv_p.append(tail(CONV_WIDTH - 1, D_POOL))

        q, _, _, rest, *kv_s = _in_proj(xs, g_pre_mix, w_in_bf, l, kv_s)
        bias_rows = jnp.repeat(b_sb[l], PAIR_ROWS).reshape(PAIR_ROWS * N_HEADS, 1)
        xp, att = _ffn_decode(xp, g_pre_ffn, wu_bf, wd_bf, g_post_ffn, l, q.astype(F32), kv_s[0], kv_s[1],
                              bias_rows, cache_k, cache_v, page_table)
        xp = _ple(xp, pp, w_ple_bf, g_ple, w_gate_bf, l)

        rest = rest.reshape(dec_b, dec_t, D_REST)
        pr, hs_buf = _mix_sample(rest.transpose(1, 0, 2), pool_t[l], conv_t[l], state_h,
                                 mix_w, l, past_len, hs_buf)
        xs = _out_proj(att, pr.transpose(1, 0, 2).reshape(n_s, -1), xs, g_grp, w_out_bf, g_post_mix, l)
        xs = _ffn(xs, g_pre_ffn, wu_bf, wd_bf, g_post_ffn, l)
        xs = _ple(xs, ps, w_ple_bf, g_ple, w_gate_bf, l)
        pool_s.append(jnp.concatenate([state_pool[l], rest[:, :, 0:D_POOL]], axis=1)[:, -POOL_BUF:])
        conv_s.append(jnp.concatenate([state_conv[l], rest[:, :, D_POOL:D_POOL + D_RNN]],
                                      axis=1)[:, -(CONV_WIDTH - 1):])
        if w_in_next:
            w_in_bf, = w_in_next

    kp, vp = kv_p
    ks, vs = kv_s
    heads = lambda a, nb: a.reshape(depth, nb, -1, N_HEADS, HEAD_DIM)
    return (xp.reshape(batch, seq, D_MODEL), xs.reshape(dec_b, dec_t, D_MODEL),
            heads(kp, batch), heads(vp, batch), jnp.stack(pool_p), jnp.stack(conv_p), hp_buf,
            heads(ks, dec_b), heads(vs, dec_b), jnp.stack(pool_s), jnp.stack(conv_s), hs_buf)
```

```python
import functools

import jax
import jax.numpy as jnp
from jax import lax
from jax.experimental import pallas as pl
from jax.experimental.pallas import tpu as pltpu

F32 = jnp.float32
BF16 = jnp.bfloat16

D_MODEL = 2048
D_ATT = 1024
HEAD_DIM = 128
N_HEADS = 8
D_POOL = 512
POOL_WINDOWS = (2, 4, 8, 16)
POOL_GROUP = 128
POOL_BUF = 15
D_RNN = 512
RNN_BLOCK = 128
CONV_WIDTH = 4
RG_C = 8.0
D_FF = 8192
D_PLE = 256
PAGE_SIZE = 128
DEC_SEQ = 4
EPS = 1e-6
D_REST = D_POOL + 2 * D_RNN
D_IN = 3 * D_ATT + D_REST
SCALE = HEAD_DIM ** -0.5

VMEM_LIMIT_V7X = 56 * 1024 * 1024
MXU_DIM_V7X = 256
SUBLANES = 8
ROW_TILE = 512
FF_TILE = 1024
ATT_TILE = 512
MIX_TILE = 256
HIST_ROWS = 16
CONV_HIST = 8
PAIR_ROWS = 2 * DEC_SEQ


def _params(sem, vmem=VMEM_LIMIT_V7X):
    return pltpu.CompilerParams(dimension_semantics=sem, vmem_limit_bytes=vmem)


def _layer_block(shape, layer):
    return pl.BlockSpec((None,) + tuple(shape), lambda *_: (layer,) + (0,) * len(shape),
                        pipeline_mode=pl.Buffered(1))


def _whole_block(shape):
    return pl.BlockSpec(tuple(shape), lambda *_: (0,) * len(shape), pipeline_mode=pl.Buffered(1))


def _row_block(width):
    return pl.BlockSpec((ROW_TILE, width), lambda i: (i, 0))


def _rms(x, g):
    return x * lax.rsqrt(jnp.mean(x * x, axis=-1, keepdims=True) + EPS) * g


def _normalize(x):
    return x * lax.rsqrt(jnp.mean(x * x, axis=-1, keepdims=True) + EPS)


def _dot(a, b):
    return jnp.dot(a, b, preferred_element_type=F32)


def _dot_nt(a, b):
    return lax.dot_general(a, b, (((1,), (1,)), ((), ())), preferred_element_type=F32)


def _softplus(x):
    return jnp.maximum(x, 0.0) + jnp.log1p(jnp.exp(-jnp.abs(x)))


def _gelu_tanh(x):
    c = 0.7978845608028654
    return 0.5 * x * (1.0 + jnp.tanh(c * (x + 0.044715 * (x * x * x))))


def _in_proj_kernel(n_alias, x_ref, g_ref, w_ref, *refs):
    q_ref, k_ref, v_ref, r_ref, kh_ref, vh_ref = refs[n_alias:]
    xn = _rms(x_ref[...], g_ref[...]).astype(BF16)
    q_ref[...] = _dot(xn, w_ref[:, 0:D_ATT]).astype(BF16)
    k = _dot(xn, w_ref[:, D_ATT:2 * D_ATT])
    v = _dot(xn, w_ref[:, 2 * D_ATT:3 * D_ATT])
    k_ref[...] = k.astype(BF16)
    v_ref[...] = v.astype(BF16)
    r_ref[...] = _dot(xn, w_ref[:, 3 * D_ATT:D_IN])
    for h in range(N_HEADS):
        kh_ref[pl.ds(h, ROW_TILE, stride=N_HEADS), :] = k[:, h * HEAD_DIM:(h + 1) * HEAD_DIM]
        vh_ref[pl.ds(h, ROW_TILE, stride=N_HEADS), :] = v[:, h * HEAD_DIM:(h + 1) * HEAD_DIM]


def _in_proj(x, g, w_bf, layer, kv_bufs):
    depth = g.shape[0]
    n = x.shape[0]
    heads = pl.BlockSpec((None, ROW_TILE * N_HEADS, HEAD_DIM), lambda i: (layer, i, 0))
    stack = jax.ShapeDtypeStruct((depth, n * N_HEADS, HEAD_DIM), F32)
    alias_in = [] if kv_bufs is None else list(kv_bufs)
    n_fixed = 3
    return pl.pallas_call(
        functools.partial(_in_proj_kernel, len(alias_in)),
        grid=(n // ROW_TILE,),
        in_specs=[_row_block(D_MODEL), _layer_block((1, D_MODEL), layer), _whole_block((D_MODEL, D_IN))]
        + [pl.BlockSpec(memory_space=pl.ANY)] * len(alias_in),
        out_specs=[_row_block(D_ATT), _row_block(D_ATT), _row_block(D_ATT), _row_block(D_REST), heads, heads],
        out_shape=[jax.ShapeDtypeStruct((n, D_ATT), BF16)] * 3 + [jax.ShapeDtypeStruct((n, D_REST), F32)]
        + [stack, stack],
        input_output_aliases={n_fixed + j: 4 + j for j in range(len(alias_in))},
        compiler_params=_params(("arbitrary",)),
        name="in_proj",
    )(x, g, w_bf, *alias_in)


def _strict_upper_ones(n):
    j = lax.broadcasted_iota(jnp.int32, (n, n), 0)
    s = lax.broadcasted_iota(jnp.int32, (n, n), 1)
    return jnp.where(j > s, 1.0, 0.0).astype(BF16)


def _sb_logs(z, mask):
    log_beta = jnp.minimum(z, 0.0) - jnp.log(1.0 + jnp.exp(-jnp.abs(z)))
    log_stay = log_beta - z
    if mask is not None:
        log_stay = jnp.where(mask, log_stay, 0.0)
    return log_beta, log_stay


def _sb_weights(log_beta, log_stay, tri, carry, mask):
    n = tri.shape[0]
    chunks = log_stay.shape[1] // n
    hi = log_stay.astype(BF16)
    lo = (log_stay - hi.astype(F32)).astype(BF16)
    tri2 = jnp.concatenate([tri, tri], axis=0)
    inner, sums = [], []
    for c in range(chunks):
        sl = slice(c * n, (c + 1) * n)
        inner.append(_dot(jnp.concatenate([hi[:, sl], lo[:, sl]], axis=1), tri2))
        sums.append(jnp.sum(log_stay[:, sl], axis=-1, keepdims=True))
    later = [None] * chunks
    for c in reversed(range(chunks)):
        later[c] = inner[c] + carry
        carry = carry + sums[c]
    w = jnp.exp(log_beta + jnp.concatenate(later, axis=1))
    if mask is not None:
        w = jnp.where(mask, w, 0.0)
    return w, carry


def _attn_prompt_kernel(n_casts, bias_ref, q_ref, k_ref, v_ref, *refs):
    o_ref = refs[n_casts]
    for w_ref, w_bf_ref in zip(refs[:n_casts], refs[n_casts + 1:], strict=True):
        w_bf_ref[...] = w_ref[...].astype(BF16)
    t = ATT_TILE
    n_blk = q_ref.shape[0] // t
    bias = bias_ref[pl.program_id(1)]
    tri = _strict_upper_ones(MXU_DIM_V7X)

    def tile(q_bf, k0, carry, mask):
        z = _dot_nt(q_bf, k_ref[k0:k0 + t, :]) * SCALE + bias
        log_beta, log_stay = _sb_logs(z, mask)
        w, carry = _sb_weights(log_beta, log_stay, tri, carry, mask)
        return _dot(w.astype(BF16), v_ref[k0:k0 + t, :]), carry

    row = lax.broadcasted_iota(jnp.int32, (t, t), 0)
    col = lax.broadcasted_iota(jnp.int32, (t, t), 1)
    diag_mask = col < row

    for qi in range(n_blk):
        q_bf = q_ref[qi * t:(qi + 1) * t, :]
        acc, carry = tile(q_bf, qi * t, jnp.zeros((t, 1), F32), diag_mask)
        for kj in reversed(range(qi)):
            out, carry = tile(q_bf, kj * t, carry, None)
            acc = acc + out
        o_ref[qi * t:(qi + 1) * t, :] = acc


def _attn_prompt(q, k, v, b_sb, casts, batch, seq):
    blk = pl.BlockSpec((seq, HEAD_DIM), lambda b, h: (b, h))
    steps = batch * N_HEADS
    cast_in, cast_out, cast_shape = [], [], []
    for w, layer in casts:
        _, rows, cols = w.shape
        assert rows % (16 * steps) == 0
        cast_in.append(pl.BlockSpec((None, rows // steps, cols),
                                    functools.partial(lambda layer, b, h: (layer, b * N_HEADS + h, 0), layer)))
        cast_out.append(pl.BlockSpec((rows // steps, cols), lambda b, h: (b * N_HEADS + h, 0)))
        cast_shape.append(jax.ShapeDtypeStruct((rows, cols), BF16))
    return pl.pallas_call(
        functools.partial(_attn_prompt_kernel, len(casts)),
        grid=(batch, N_HEADS),
        in_specs=[pl.BlockSpec(memory_space=pltpu.SMEM), blk, blk, blk] + cast_in,
        out_specs=[blk] + cast_out,
        out_shape=[jax.ShapeDtypeStruct((batch * seq, D_ATT), F32)] + cast_shape,
        compiler_params=_params(("arbitrary", "arbitrary")),
        name="attn_prompt",
    )(b_sb, q, k, v, *[w for w, _ in casts])


def _attn_sample_kernel(n_pages, pt_ref, bias_ref, q_ref, kn_ref, vn_ref, *rest):
    del pt_ref
    k_refs = rest[:n_pages]
    v_refs = rest[n_pages:2 * n_pages]
    o_ref, kpad_ref, vpad_ref = rest[2 * n_pages:]
    b = pl.program_id(0)
    rows = PAIR_ROWS * N_HEADS
    t = PAGE_SIZE

    def keys_by_heads(ref, n_keys):
        return jnp.concatenate([ref[pl.ds(h, n_keys, stride=N_HEADS), :] for h in range(N_HEADS)], axis=1)

    @pl.when(b == 0)
    def _():
        kpad_ref[...] = jnp.zeros_like(kpad_ref)
        vpad_ref[...] = jnp.zeros_like(vpad_ref)

    kpad_ref[0:PAIR_ROWS, :] = keys_by_heads(kn_ref, PAIR_ROWS)
    vpad_ref[0:PAIR_ROWS, :] = keys_by_heads(vn_ref, PAIR_ROWS)

    q_rep = jnp.concatenate([q_ref[0]] * N_HEADS, axis=0)
    r_id = lax.broadcasted_iota(jnp.int32, (rows, D_ATT), 0)
    c_id = lax.broadcasted_iota(jnp.int32, (rows, D_ATT), 1)
    q_bd = jnp.where(c_id // HEAD_DIM == r_id // PAIR_ROWS, q_rep, 0.0).astype(BF16)

    z = [_dot_nt(q_bd, keys_by_heads(k_refs[p], t).astype(BF16)) for p in range(n_pages)]
    z.append(_dot_nt(q_bd, kpad_ref[...].astype(BF16)))
    z = jnp.concatenate(z, axis=1) * SCALE + bias_ref[...]
    row = lax.broadcasted_iota(jnp.int32, z.shape, 0) % PAIR_ROWS
    col = lax.broadcasted_iota(jnp.int32, z.shape, 1) - n_pages * t
    mask = (col < 0) | ((col < PAIR_ROWS) & (col // DEC_SEQ == row // DEC_SEQ) & (col % DEC_SEQ < row % DEC_SEQ))
    log_beta, log_stay = _sb_logs(z, mask)
    w, _ = _sb_weights(log_beta, log_stay, _strict_upper_ones(t), jnp.zeros((rows, 1), F32), mask)
    w = w.astype(BF16)
    acc = _dot(w[:, n_pages * t:], vpad_ref[...].astype(BF16))
    for p in range(n_pages):
        acc = acc + _dot(w[:, p * t:(p + 1) * t], keys_by_heads(v_refs[p], t).astype(BF16))

    o8 = jnp.concatenate(
        [acc[h * PAIR_ROWS:(h + 1) * PAIR_ROWS, h * HEAD_DIM:(h + 1) * HEAD_DIM] for h in range(N_HEADS)], axis=1)

    @pl.when(b % 2 == 0)
    def _():
        o_ref[0, 0:DEC_SEQ, :] = o8[0:DEC_SEQ]

    @pl.when(b % 2 == 1)
    def _():
        o_ref[0, DEC_SEQ:PAIR_ROWS, :] = o8[DEC_SEQ:PAIR_ROWS]


def _attn_sample(q_s, k_new, v_new, layer, bias_rows, cache_k, cache_v, page_table):
    dec_b, n_pages = page_table.shape
    depth, n_pool = cache_k.shape[:2]
    flat_cache = lambda c: c.reshape(depth, n_pool, PAGE_SIZE * N_HEADS, HEAD_DIM)
    pair_spec = pl.BlockSpec((1, PAIR_ROWS, D_ATT), lambda b, pt: (b // 2, 0, 0))
    new_spec = pl.BlockSpec((None, PAIR_ROWS * N_HEADS, HEAD_DIM), lambda b, pt: (layer, b // 2, 0))

    def page_spec(p):
        return pl.BlockSpec((None, None, PAGE_SIZE * N_HEADS, HEAD_DIM), lambda b, pt: (layer, pt[b, p], 0, 0))

    grid_spec = pltpu.PrefetchScalarGridSpec(
        num_scalar_prefetch=1,
        grid=(dec_b,),
        in_specs=[pl.BlockSpec((PAIR_ROWS * N_HEADS, 1), lambda b, pt: (0, 0)), pair_spec, new_spec, new_spec]
        + [page_spec(p) for p in range(n_pages)] * 2,
        out_specs=pair_spec,
        scratch_shapes=[pltpu.VMEM((PAGE_SIZE, D_ATT), F32), pltpu.VMEM((PAGE_SIZE, D_ATT), F32)],
    )
    out = pl.pallas_call(
        functools.partial(_attn_sample_kernel, n_pages),
        grid_spec=grid_spec,
        out_shape=jax.ShapeDtypeStruct((dec_b // 2, PAIR_ROWS, D_ATT), F32),
        compiler_params=_params(("arbitrary",)),
        name="attn_sample",
    )(page_table, bias_rows, q_s.reshape(dec_b // 2, PAIR_ROWS, D_ATT), k_new, v_new,
      *([flat_cache(cache_k)] * n_pages), *([flat_cache(cache_v)] * n_pages))
    return out.reshape(dec_b * DEC_SEQ, D_ATT)


def _pool_project(pooled, wp_ref, sp):
    outs = []
    for g in range(len(POOL_WINDOWS)):
        sl = slice(g * POOL_GROUP, (g + 1) * POOL_GROUP)
        outs.append(_dot(pooled[:, sl].astype(BF16), wp_ref[g]))
    return jnp.concatenate(outs, axis=1) * sp


def _rglru_inputs(xc, wa_ref, ba, wx_ref, bx, sp_lam):
    ra, ia = [], []
    for g in range(D_RNN // RNN_BLOCK):
        xg = xc[:, g * RNN_BLOCK:(g + 1) * RNN_BLOCK].astype(BF16)
        ra.append(_dot(xg, wa_ref[g]))
        ia.append(_dot(xg, wx_ref[g]))
    r = jax.nn.sigmoid(jnp.concatenate(ra, axis=1) + ba)
    i = jax.nn.sigmoid(jnp.concatenate(ia, axis=1) + bx)
    log_a = -RG_C * r * sp_lam
    a = jnp.exp(log_a)
    inp = jnp.sqrt(-jnp.tanh(log_a) * (a * a + 1.0)) * (i * xc)
    return a, inp


def _mix_prompt_kernel(nb, n_alias, *refs):
    rest_refs = refs[:nb]
    wp_ref, sp_ref, cw_ref, cb_ref, wa_ref, ba_ref, wx_ref, bx_ref, lam_ref = refs[nb:nb + 9]
    o_ref, h_ref, pool_ext, conv_ext, a_scr, b_scr, hs_scr, h_scr = refs[nb + 9 + n_alias:]
    tt = MIX_TILE
    step = pl.program_id(0)

    @pl.when(step == 0)
    def _():
        pool_ext[:, 0:HIST_ROWS, :] = jnp.zeros((nb, HIST_ROWS, D_POOL), F32)
        conv_ext[:, 0:CONV_HIST, :] = jnp.zeros((nb, CONV_HIST, D_RNN), F32)
        h_scr[...] = jnp.zeros_like(h_scr)

    pos = step * tt + lax.broadcasted_iota(jnp.int32, (tt, POOL_GROUP), 0)
    sp_lam = _softplus(-lam_ref[...])
    for b in range(nb):
        rest_ref = rest_refs[b]
        u = rest_ref[:, 0:D_POOL]
        pool_ext[b, HIST_ROWS:HIST_ROWS + tt, :] = u
        pooled = []
        for g, win in enumerate(POOL_WINDOWS):
            sl = slice(g * POOL_GROUP, (g + 1) * POOL_GROUP)
            s = u[:, sl]
            for j in range(1, win):
                s = s + pool_ext[b, HIST_ROWS - j:HIST_ROWS - j + tt, sl]
            cnt = jnp.minimum(win, pos + 1).astype(F32)
            pooled.append(s / cnt - u[:, sl])
        o_ref[b, :, 0:D_POOL] = _pool_project(jnp.concatenate(pooled, axis=1), wp_ref, sp_ref[...])
        pool_ext[b, 0:HIST_ROWS, :] = pool_ext[b, tt:tt + HIST_ROWS, :]

        ur = rest_ref[:, D_POOL:D_POOL + D_RNN]
        conv_ext[b, CONV_HIST:CONV_HIST + tt, :] = ur
        xc = cb_ref[...]
        for j in range(CONV_WIDTH):
            off = CONV_HIST - (CONV_WIDTH - 1) + j
            xc = xc + conv_ext[b, off:off + tt, :] * cw_ref[j:j + 1, :]
        conv_ext[b, 0:CONV_HIST, :] = conv_ext[b, tt:tt + CONV_HIST, :]
        a, inp = _rglru_inputs(xc, wa_ref, ba_ref[...], wx_ref, bx_ref[...], sp_lam)
        a_scr[b] = a
        b_scr[b] = inp

    def scan_row(t, hs):
        new = []
        for b in range(nb):
            h = a_scr[b, pl.ds(t, 1), :] * hs[b] + b_scr[b, pl.ds(t, 1), :]
            hs_scr[b, pl.ds(t, 1), :] = h
            new.append(h)
        return tuple(new)

    hs = lax.fori_loop(0, tt, scan_row, tuple(h_scr[b:b + 1, :] for b in range(nb)), unroll=8)
    for b in range(nb):
        h_scr[b:b + 1, :] = hs[b]
        o_ref[b, :, D_POOL:D_POOL + D_RNN] = hs_scr[b] * _gelu_tanh(rest_refs[b][:, D_POOL + D_RNN:D_REST])
    h_ref[0] = h_scr[0:nb, :]


def _mix_weight_specs(layer):
    n_blk = D_RNN // RNN_BLOCK
    return [_layer_block((len(POOL_WINDOWS), POOL_GROUP, POOL_GROUP), layer), _layer_block((1, D_POOL), layer),
            _layer_block((CONV_WIDTH, D_RNN), layer), _layer_block((1, D_RNN), layer),
            _layer_block((n_blk, RNN_BLOCK, RNN_BLOCK), layer), _layer_block((1, D_RNN), layer),
            _layer_block((n_blk, RNN_BLOCK, RNN_BLOCK), layer), _layer_block((1, D_RNN), layer),
            _layer_block((1, D_RNN), layer)]


def _mix_prompt(rest, mix_w, layer, batch, seq, h_buf):
    tt = MIX_TILE
    depth = mix_w[0].shape[0]
    steps = seq // tt
    alias_in = [] if h_buf is None else [h_buf]
    rest_specs = [pl.BlockSpec((tt, D_REST), functools.partial(lambda b, i: (b * steps + i, 0), b))
                  for b in range(batch)]
    return pl.pallas_call(
        functools.partial(_mix_prompt_kernel, batch, len(alias_in)),
        grid=(steps,),
        in_specs=rest_specs + _mix_weight_specs(layer) + [pl.BlockSpec(memory_space=pl.ANY)] * len(alias_in),
        out_specs=[pl.BlockSpec((batch, tt, D_POOL + D_RNN), lambda i: (0, i, 0)),
                   pl.BlockSpec((1, batch, D_RNN), lambda i: (layer, 0, 0))],
        out_shape=[jax.ShapeDtypeStruct((batch, seq, D_POOL + D_RNN), F32),
                   jax.ShapeDtypeStruct((depth, batch, D_RNN), F32)],
        scratch_shapes=[pltpu.VMEM((batch, HIST_ROWS + tt, D_POOL), F32),
                        pltpu.VMEM((batch, CONV_HIST + tt, D_RNN), F32),
                        pltpu.VMEM((batch, tt, D_RNN), F32),
                        pltpu.VMEM((batch, tt, D_RNN), F32),
                        pltpu.VMEM((batch, tt, D_RNN), F32),
                        pltpu.VMEM((SUBLANES, D_RNN), F32)],
        input_output_aliases={batch + len(mix_w) + j: 1 + j for j in range(len(alias_in))},
        compiler_params=_params(("arbitrary",)),
        name="mix_prompt",
    )(*([rest] * batch), *mix_w, *alias_in)


def _mix_sample_kernel(pos0, n_alias, rest_ref, pool_ref, conv_ref, h0_ref,
                       wp_ref, sp_ref, cw_ref, cb_ref, wa_ref, ba_ref, wx_ref, bx_ref, lam_ref, *refs):
    o_ref, h_ref = refs[n_alias:]
    steps = rest_ref.shape[0]
    sp_lam = _softplus(-lam_ref[...])
    pool_rows = [pool_ref[j] for j in range(POOL_BUF)] + [rest_ref[i, :, 0:D_POOL] for i in range(steps)]
    conv_rows = ([conv_ref[j] for j in range(CONV_WIDTH - 1)]
                 + [rest_ref[i, :, D_POOL:D_POOL + D_RNN] for i in range(steps)])
    h = h0_ref[...]
    for i in range(steps):
        u = pool_rows[POOL_BUF + i]
        pooled = []
        for g, win in enumerate(POOL_WINDOWS):
            sl = slice(g * POOL_GROUP, (g + 1) * POOL_GROUP)
            s = u[:, sl]
            for j in range(1, win):
                s = s + pool_rows[POOL_BUF + i - j][:, sl]
            pooled.append(s / float(min(win, pos0 + i + 1)) - u[:, sl])
        o_ref[i, :, 0:D_POOL] = _pool_project(jnp.concatenate(pooled, axis=1), wp_ref, sp_ref[...])

        xc = cb_ref[...]
        for j in range(CONV_WIDTH):
            xc = xc + conv_rows[i + j] * cw_ref[j:j + 1, :]
        a, inp = _rglru_inputs(xc, wa_ref, ba_ref[...], wx_ref, bx_ref[...], sp_lam)
        h = a * h + inp
        o_ref[i, :, D_POOL:D_POOL + D_RNN] = h * _gelu_tanh(rest_ref[i, :, D_POOL + D_RNN:D_REST])
    h_ref[0] = h


def _mix_sample(rest_t, pool_t, conv_t, state_h, mix_w, layer, pos0, h_buf):
    steps, dec_b, _ = rest_t.shape
    depth = state_h.shape[0]
    whole = lambda shape: pl.BlockSpec(shape, lambda i: (0,) * len(shape))
    alias_in = [] if h_buf is None else [h_buf]
    n_in = 4 + len(mix_w)
    return pl.pallas_call(
        functools.partial(_mix_sample_kernel, pos0, len(alias_in)),
        grid=(1,),
        in_specs=[whole(rest_t.shape), whole(pool_t.shape), whole(conv_t.shape),
                  pl.BlockSpec((None, dec_b, D_RNN), lambda i: (layer, 0, 0))]
        + _mix_weight_specs(layer) + [pl.BlockSpec(memory_space=pl.ANY)] * len(alias_in),
        out_specs=[whole((steps, dec_b, D_POOL + D_RNN)),
                   pl.BlockSpec((1, dec_b, D_RNN), lambda i: (layer, 0, 0))],
        out_shape=[jax.ShapeDtypeStruct((steps, dec_b, D_POOL + D_RNN), F32),
                   jax.ShapeDtypeStruct((depth, dec_b, D_RNN), F32)],
        input_output_aliases={n_in + j: 1 + j for j in range(len(alias_in))},
        compiler_params=_params(("arbitrary",)),
        name="mix_sample",
    )(rest_t, pool_t, conv_t, state_h, *mix_w, *alias_in)


def _out_proj_kernel(att_ref, pr_ref, x_ref, gg_ref, w_ref, gp_ref, o_ref):
    pr = pr_ref[...]
    mixed = jnp.concatenate(
        [_normalize(att_ref[...]), _normalize(pr[:, 0:D_POOL]), _normalize(pr[:, D_POOL:])],
        axis=1) * gg_ref[...]
    y = _dot(mixed.astype(BF16), w_ref[...])
    o_ref[...] = x_ref[...] + _rms(y, gp_ref[...])


def _out_proj(att, pr, x, g_grp, w_bf, g_post, layer):
    n = x.shape[0]
    return pl.pallas_call(
        _out_proj_kernel,
        grid=(n // ROW_TILE,),
        in_specs=[_row_block(D_ATT), _row_block(D_POOL + D_RNN), _row_block(D_MODEL),
                  _layer_block((1, D_MODEL), layer), _whole_block((D_MODEL, D_MODEL)),
                  _layer_block((1, D_MODEL), layer)],
        out_specs=_row_block(D_MODEL),
        out_shape=jax.ShapeDtypeStruct((n, D_MODEL), F32),
        compiler_params=_params(("arbitrary",)),
        name="out_proj",
    )(att, pr, x, g_grp, w_bf, g_post)


def _ffn_kernel(x_ref, g1_ref, wu_ref, wd_ref, g2_ref, o_ref, xn_scr, acc_scr):
    f = pl.program_id(1)

    @pl.when(f == 0)
    def _():
        xn_scr[...] = _rms(x_ref[...], g1_ref[...]).astype(BF16)
        acc_scr[...] = jnp.zeros_like(acc_scr)

    h = jnp.square(jnp.maximum(_dot(xn_scr[...], wu_ref[...]), 0.0))
    acc_scr[...] += _dot(h.astype(BF16), wd_ref[...])

    @pl.when(f == pl.num_programs(1) - 1)
    def _():
        o_ref[...] = x_ref[...] + _rms(acc_scr[...], g2_ref[...])


def _ffn(x, g_pre, wu_bf, wd_bf, g_post, layer):
    n = x.shape[0]
    row = pl.BlockSpec((ROW_TILE, D_MODEL), lambda i, f: (i, 0))
    return pl.pallas_call(
        _ffn_kernel,
        grid=(n // ROW_TILE, D_FF // FF_TILE),
        in_specs=[row, _layer_block((1, D_MODEL), layer),
                  pl.BlockSpec((D_MODEL, FF_TILE), lambda i, f: (0, f)),
                  pl.BlockSpec((FF_TILE, D_MODEL), lambda i, f: (f, 0)),
                  _layer_block((1, D_MODEL), layer)],
        out_specs=row,
        out_shape=jax.ShapeDtypeStruct((n, D_MODEL), F32),
        scratch_shapes=[pltpu.VMEM((ROW_TILE, D_MODEL), BF16), pltpu.VMEM((ROW_TILE, D_MODEL), F32)],
        compiler_params=_params(("arbitrary", "arbitrary")),
        name="ffn",
    )(x, g_pre, wu_bf, wd_bf, g_post)


def _ple_kernel(x_ref, p_ref, wp_ref, g_ref, wg_ref, o_ref):
    x = x_ref[...]
    e = _rms(_dot(p_ref[...].astype(BF16), wp_ref[...]), g_ref[...])
    o_ref[...] = x + e * jax.nn.sigmoid(_dot(x.astype(BF16), wg_ref[...]))


def _ple(x, p, wp_bf, g, wg_bf, layer):
    n = x.shape[0]
    return pl.pallas_call(
        _ple_kernel,
        grid=(n // ROW_TILE,),
        in_specs=[_row_block(D_MODEL), pl.BlockSpec((None, ROW_TILE, D_PLE), lambda i: (layer, i, 0)),
                  _layer_block((D_PLE, D_MODEL), layer), _layer_block((1, D_MODEL), layer),
                  _whole_block((D_MODEL, D_MODEL))],
        out_specs=_row_block(D_MODEL),
        out_shape=jax.ShapeDtypeStruct((n, D_MODEL), F32),
        compiler_params=_params(("arbitrary",)),
        name="ple",
    )(x, p, wp_bf, g, wg_bf)


def kernel(x_prompt, x_sample, p_prompt, p_sample, cache_k, cache_v, page_table, state_pool, state_conv, state_h, g_pre_mix, w_in, b_sb, w_pool, s_pool, conv_w, conv_b, w_a, b_a, w_x, b_x, lam, g_grp, w_out, g_post_mix, g_pre_ffn, w_up, w_down, g_post_ffn, w_ple, g_ple, w_ple_gate):
    depth = w_in.shape[0]
    batch, seq, _ = x_prompt.shape
    dec_b, dec_t, _ = x_sample.shape
    n_p, n_s = batch * seq, dec_b * dec_t
    past_len = page_table.shape[1] * PAGE_SIZE
    assert dec_t == DEC_SEQ and dec_b % 2 == 0 and n_p % ROW_TILE == 0 and n_s % ROW_TILE == 0
    assert seq % MIX_TILE == 0 and seq % ATT_TILE == 0 and past_len + dec_t > max(POOL_WINDOWS)

    bf = lambda a: a.astype(BF16)
    vec = lambda a: a.reshape(depth, 1, -1)
    w_in_bf = bf(w_in[0])
    w_ple_bf = bf(w_ple)
    mix_w = (bf(w_pool), vec(s_pool), conv_w, vec(conv_b), bf(w_a), vec(b_a), bf(w_x), vec(b_x), vec(lam))
    g_pre_mix, g_grp, g_post_mix, g_pre_ffn, g_post_ffn, g_ple = map(
        vec, (g_pre_mix, g_grp, g_post_mix, g_pre_ffn, g_post_ffn, g_ple))
    pp = p_prompt.reshape(depth, n_p, D_PLE)
    ps = p_sample.reshape(depth, n_s, D_PLE)
    pool_t = state_pool.transpose(0, 2, 1, 3)
    conv_t = state_conv.transpose(0, 2, 1, 3)

    xp = x_prompt.reshape(n_p, D_MODEL)
    xs = x_sample.reshape(n_s, D_MODEL)
    kv_p = kv_s = hp_buf = hs_buf = None
    pool_p, conv_p, pool_s, conv_s = [], [], [], []

    def dense_tail(att, pr, x, p, w_bf, l):
        w_out_bf, wu_bf, wd_bf, w_gate_bf = w_bf
        x = _out_proj(att, pr, x, g_grp, w_out_bf, g_post_mix, l)
        x = _ffn(x, g_pre_ffn, wu_bf, wd_bf, g_post_ffn, l)
        return _ple(x, p, w_ple_bf, g_ple, w_gate_bf, l)

    for l in range(depth):
        q, k, v, rest, *kv_p = _in_proj(xp, g_pre_mix, w_in_bf, l, kv_p)
        casts = [(w_out, l), (w_up, l), (w_down, l), (w_ple_gate, l)] + ([(w_in, l + 1)] if l + 1 < depth else [])
        att, *w_bf = _attn_prompt(q, k, v, b_sb[l], casts, batch, seq)
        w_in_next = w_bf.pop() if l + 1 < depth else None
        pr, hp_buf = _mix_prompt(rest, mix_w, l, batch, seq, hp_buf)
        xp = dense_tail(att, pr.reshape(n_p, -1), xp, pp, w_bf, l)
        tail = lambda rows, lo: jnp.stack(
            [rest[(b + 1) * seq - rows:(b + 1) * seq, lo:lo + D_POOL] for b in range(batch)])
        pool_p.append(tail(POOL_BUF, 0))
        conv_p.append(tail(CONV_WIDTH - 1, D_POOL))

        q, _, _, rest, *kv_s = _in_proj(xs, g_pre_mix, w_in_bf, l, kv_s)
        bias_rows = jnp.repeat(b_sb[l], PAIR_ROWS).reshape(PAIR_ROWS * N_HEADS, 1)
        att = _attn_sample(q.astype(F32), kv_s[0], kv_s[1], l, bias_rows, cache_k, cache_v, page_table)
        rest = rest.reshape(dec_b, dec_t, D_REST)
        pr, hs_buf = _mix_sample(rest.transpose(1, 0, 2), pool_t[l], conv_t[l], state_h,
                                 mix_w, l, past_len, hs_buf)
        xs = dense_tail(att, pr.transpose(1, 0, 2).reshape(n_s, -1), xs, ps, w_bf, l)
        pool_s.append(jnp.concatenate([state_pool[l], rest[:, :, 0:D_POOL]], axis=1)[:, -POOL_BUF:])
        conv_s.append(jnp.concatenate([state_conv[l], rest[:, :, D_POOL:D_POOL + D_RNN]],
                                      axis=1)[:, -(CONV_WIDTH - 1):])
        w_in_bf = w_in_next

    kp, vp = kv_p
    ks, vs = kv_s
    heads = lambda a, nb: a.reshape(depth, nb, -1, N_HEADS, HEAD_DIM)
    return (xp.reshape(batch, seq, D_MODEL), xs.reshape(dec_b, dec_t, D_MODEL),
            heads(kp, batch), heads(vp, batch), jnp.stack(pool_p), jnp.stack(conv_p), hp_buf,
            heads(ks, dec_b), heads(vs, dec_b), jnp.stack(pool_s), jnp.stack(conv_s), hs_buf)
```

```python
import functools

import jax
import jax.numpy as jnp
from jax import lax
from jax.experimental import pallas as pl
from jax.experimental.pallas import tpu as pltpu

F32 = jnp.float32
BF16 = jnp.bfloat16

D_MODEL = 2048
D_ATT = 1024
HEAD_DIM = 128
N_HEADS = 8
D_POOL = 512
POOL_WINDOWS = (2, 4, 8, 16)
POOL_GROUP = 128
POOL_BUF = 15
D_RNN = 512
RNN_BLOCK = 128
CONV_WIDTH = 4
RG_C = 8.0
D_FF = 8192
D_PLE = 256
PAGE_SIZE = 128
DEC_SEQ = 4
EPS = 1e-6
D_REST = D_POOL + 2 * D_RNN
D_IN = 3 * D_ATT + D_REST
SCALE = HEAD_DIM ** -0.5

VMEM_LIMIT_V7X = 56 * 1024 * 1024
MXU_DIM_V7X = 256
SUBLANES = 8
ROW_TILE = 512
FF_TILE = 2048
ATT_TILE = 512
MIX_TILE = 256
HIST_ROWS = 16
CONV_HIST = 8
PAIR_ROWS = 2 * DEC_SEQ


def _params(sem, vmem=VMEM_LIMIT_V7X):
    return pltpu.CompilerParams(dimension_semantics=sem, vmem_limit_bytes=vmem)


def _layer_block(shape, layer):
    return pl.BlockSpec((None,) + tuple(shape), lambda *_: (layer,) + (0,) * len(shape),
                        pipeline_mode=pl.Buffered(1))


def _whole_block(shape):
    return pl.BlockSpec(tuple(shape), lambda *_: (0,) * len(shape), pipeline_mode=pl.Buffered(1))


def _row_block(width):
    return pl.BlockSpec((ROW_TILE, width), lambda i: (i, 0))


def _rms(x, g):
    return x * lax.rsqrt(jnp.mean(x * x, axis=-1, keepdims=True) + EPS) * g


def _normalize(x):
    return x * lax.rsqrt(jnp.mean(x * x, axis=-1, keepdims=True) + EPS)


def _dot(a, b):
    return jnp.dot(a, b, preferred_element_type=F32)


def _dot_nt(a, b):
    return lax.dot_general(a, b, (((1,), (1,)), ((), ())), preferred_element_type=F32)


def _softplus(x):
    return jnp.maximum(x, 0.0) + jnp.log1p(jnp.exp(-jnp.abs(x)))


def _gelu_tanh(x):
    c = 0.7978845608028654
    return 0.5 * x * (1.0 + jnp.tanh(c * (x + 0.044715 * (x * x * x))))


def _in_proj_kernel(n_alias, x_ref, g_ref, w_ref, *refs):
    q_ref, k_ref, v_ref, r_ref, kh_ref, vh_ref = refs[n_alias:]
    xn = _rms(x_ref[...], g_ref[...]).astype(BF16)
    q_ref[...] = _dot(xn, w_ref[:, 0:D_ATT]).astype(BF16)
    k = _dot(xn, w_ref[:, D_ATT:2 * D_ATT])
    v = _dot(xn, w_ref[:, 2 * D_ATT:3 * D_ATT])
    k_ref[...] = k.astype(BF16)
    v_ref[...] = v.astype(BF16)
    r_ref[...] = _dot(xn, w_ref[:, 3 * D_ATT:D_IN])
    for h in range(N_HEADS):
        kh_ref[pl.ds(h, ROW_TILE, stride=N_HEADS), :] = k[:, h * HEAD_DIM:(h + 1) * HEAD_DIM]
        vh_ref[pl.ds(h, ROW_TILE, stride=N_HEADS), :] = v[:, h * HEAD_DIM:(h + 1) * HEAD_DIM]


def _in_proj(x, g, w_bf, layer, kv_bufs):
    depth = g.shape[0]
    n = x.shape[0]
    heads = pl.BlockSpec((None, ROW_TILE * N_HEADS, HEAD_DIM), lambda i: (layer, i, 0))
    stack = jax.ShapeDtypeStruct((depth, n * N_HEADS, HEAD_DIM), F32)
    alias_in = [] if kv_bufs is None else list(kv_bufs)
    n_fixed = 3
    return pl.pallas_call(
        functools.partial(_in_proj_kernel, len(alias_in)),
        grid=(n // ROW_TILE,),
        in_specs=[_row_block(D_MODEL), _layer_block((1, D_MODEL), layer), _whole_block((D_MODEL, D_IN))]
        + [pl.BlockSpec(memory_space=pl.ANY)] * len(alias_in),
        out_specs=[_row_block(D_ATT), _row_block(D_ATT), _row_block(D_ATT), _row_block(D_REST), heads, heads],
        out_shape=[jax.ShapeDtypeStruct((n, D_ATT), BF16)] * 3 + [jax.ShapeDtypeStruct((n, D_REST), F32)]
        + [stack, stack],
        input_output_aliases={n_fixed + j: 4 + j for j in range(len(alias_in))},
        compiler_params=_params(("arbitrary",)),
        name="in_proj",
    )(x, g, w_bf, *alias_in)


def _strict_upper_ones(n):
    j = lax.broadcasted_iota(jnp.int32, (n, n), 0)
    s = lax.broadcasted_iota(jnp.int32, (n, n), 1)
    return jnp.where(j > s, 1.0, 0.0).astype(BF16)


def _sb_logs(z, mask):
    log_beta = jnp.minimum(z, 0.0) - jnp.log(1.0 + jnp.exp(-jnp.abs(z)))
    log_stay = log_beta - z
    if mask is not None:
        log_stay = jnp.where(mask, log_stay, 0.0)
    return log_beta, log_stay


def _sb_weights(log_beta, log_stay, tri, carry, mask):
    n = tri.shape[0]
    chunks = log_stay.shape[1] // n
    hi = log_stay.astype(BF16)
    lo = (log_stay - hi.astype(F32)).astype(BF16)
    tri2 = jnp.concatenate([tri, tri], axis=0)
    inner, sums = [], []
    for c in range(chunks):
        sl = slice(c * n, (c + 1) * n)
        inner.append(_dot(jnp.concatenate([hi[:, sl], lo[:, sl]], axis=1), tri2))
        sums.append(jnp.sum(log_stay[:, sl], axis=-1, keepdims=True))
    later = [None] * chunks
    for c in reversed(range(chunks)):
        later[c] = inner[c] + carry
        carry = carry + sums[c]
    w = jnp.exp(log_beta + jnp.concatenate(later, axis=1))
    if mask is not None:
        w = jnp.where(mask, w, 0.0)
    return w, carry


def _attn_prompt_kernel(n_casts, bias_ref, q_ref, k_ref, v_ref, *refs):
    o_ref = refs[n_casts]
    for w_ref, w_bf_ref in zip(refs[:n_casts], refs[n_casts + 1:], strict=True):
        w_bf_ref[...] = w_ref[...].astype(BF16)
    t = ATT_TILE
    n_blk = q_ref.shape[0] // t
    bias = bias_ref[pl.program_id(1)]
    tri = _strict_upper_ones(MXU_DIM_V7X)

    def tile(q_bf, k0, carry, mask):
        z = _dot_nt(q_bf, k_ref[k0:k0 + t, :]) * SCALE + bias
        log_beta, log_stay = _sb_logs(z, mask)
        w, carry = _sb_weights(log_beta, log_stay, tri, carry, mask)
        return _dot(w.astype(BF16), v_ref[k0:k0 + t, :]), carry

    row = lax.broadcasted_iota(jnp.int32, (t, t), 0)
    col = lax.broadcasted_iota(jnp.int32, (t, t), 1)
    diag_mask = col < row

    for qi in range(n_blk):
        q_bf = q_ref[qi * t:(qi + 1) * t, :]
        acc, carry = tile(q_bf, qi * t, jnp.zeros((t, 1), F32), diag_mask)
        for kj in reversed(range(qi)):
            out, carry = tile(q_bf, kj * t, carry, None)
            acc = acc + out
        o_ref[qi * t:(qi + 1) * t, :] = acc


def _attn_prompt(q, k, v, b_sb, casts, batch, seq):
    blk = pl.BlockSpec((seq, HEAD_DIM), lambda b, h: (b, h))
    steps = batch * N_HEADS
    cast_in, cast_out, cast_shape = [], [], []
    for w, layer in casts:
        _, rows, cols = w.shape
        assert rows % (16 * steps) == 0
        cast_in.append(pl.BlockSpec((None, rows // steps, cols),
                                    functools.partial(lambda layer, b, h: (layer, b * N_HEADS + h, 0), layer)))
        cast_out.append(pl.BlockSpec((rows // steps, cols), lambda b, h: (b * N_HEADS + h, 0)))
        cast_shape.append(jax.ShapeDtypeStruct((rows, cols), BF16))
    return pl.pallas_call(
        functools.partial(_attn_prompt_kernel, len(casts)),
        grid=(batch, N_HEADS),
        in_specs=[pl.BlockSpec(memory_space=pltpu.SMEM), blk, blk, blk] + cast_in,
        out_specs=[blk] + cast_out,
        out_shape=[jax.ShapeDtypeStruct((batch * seq, D_ATT), F32)] + cast_shape,
        compiler_params=_params(("arbitrary", "arbitrary")),
        name="attn_prompt",
    )(b_sb, q, k, v, *[w for w, _ in casts])


def _attn_sample_kernel(n_pages, pt_ref, bias_ref, q_ref, kn_ref, vn_ref, *rest):
    del pt_ref
    k_refs = rest[:n_pages]
    v_refs = rest[n_pages:2 * n_pages]
    o_ref, kpad_ref, vpad_ref = rest[2 * n_pages:]
    b = pl.program_id(0)
    rows = PAIR_ROWS * N_HEADS
    t = PAGE_SIZE

    def keys_by_heads(ref, n_keys):
        return jnp.concatenate([ref[pl.ds(h, n_keys, stride=N_HEADS), :] for h in range(N_HEADS)], axis=1)

    @pl.when(b == 0)
    def _():
        kpad_ref[...] = jnp.zeros_like(kpad_ref)
        vpad_ref[...] = jnp.zeros_like(vpad_ref)

    kpad_ref[0:PAIR_ROWS, :] = keys_by_heads(kn_ref, PAIR_ROWS)
    vpad_ref[0:PAIR_ROWS, :] = keys_by_heads(vn_ref, PAIR_ROWS)

    q_rep = jnp.concatenate([q_ref[0]] * N_HEADS, axis=0)
    r_id = lax.broadcasted_iota(jnp.int32, (rows, D_ATT), 0)
    c_id = lax.broadcasted_iota(jnp.int32, (rows, D_ATT), 1)
    q_bd = jnp.where(c_id // HEAD_DIM == r_id // PAIR_ROWS, q_rep, 0.0).astype(BF16)

    z = [_dot_nt(q_bd, keys_by_heads(k_refs[p], t).astype(BF16)) for p in range(n_pages)]
    z.append(_dot_nt(q_bd, kpad_ref[...].astype(BF16)))
    z = jnp.concatenate(z, axis=1) * SCALE + bias_ref[...]
    row = lax.broadcasted_iota(jnp.int32, z.shape, 0) % PAIR_ROWS
    col = lax.broadcasted_iota(jnp.int32, z.shape, 1) - n_pages * t
    mask = (col < 0) | ((col < PAIR_ROWS) & (col // DEC_SEQ == row // DEC_SEQ) & (col % DEC_SEQ < row % DEC_SEQ))
    log_beta, log_stay = _sb_logs(z, mask)
    w, _ = _sb_weights(log_beta, log_stay, _strict_upper_ones(t), jnp.zeros((rows, 1), F32), mask)
    w = w.astype(BF16)
    acc = _dot(w[:, n_pages * t:], vpad_ref[...].astype(BF16))
    for p in range(n_pages):
        acc = acc + _dot(w[:, p * t:(p + 1) * t], keys_by_heads(v_refs[p], t).astype(BF16))

    o8 = jnp.concatenate(
        [acc[h * PAIR_ROWS:(h + 1) * PAIR_ROWS, h * HEAD_DIM:(h + 1) * HEAD_DIM] for h in range(N_HEADS)], axis=1)

    @pl.when(b % 2 == 0)
    def _():
        o_ref[0, 0:DEC_SEQ, :] = o8[0:DEC_SEQ]

    @pl.when(b % 2 == 1)
    def _():
        o_ref[0, DEC_SEQ:PAIR_ROWS, :] = o8[DEC_SEQ:PAIR_ROWS]


def _attn_sample(q_s, k_new, v_new, layer, bias_rows, cache_k, cache_v, page_table):
    dec_b, n_pages = page_table.shape
    depth, n_pool = cache_k.shape[:2]
    flat_cache = lambda c: c.reshape(depth, n_pool, PAGE_SIZE * N_HEADS, HEAD_DIM)
    pair_spec = pl.BlockSpec((1, PAIR_ROWS, D_ATT), lambda b, pt: (b // 2, 0, 0))
    new_spec = pl.BlockSpec((None, PAIR_ROWS * N_HEADS, HEAD_DIM), lambda b, pt: (layer, b // 2, 0))

    def page_spec(p):
        return pl.BlockSpec((None, None, PAGE_SIZE * N_HEADS, HEAD_DIM), lambda b, pt: (layer, pt[b, p], 0, 0))

    grid_spec = pltpu.PrefetchScalarGridSpec(
        num_scalar_prefetch=1,
        grid=(dec_b,),
        in_specs=[pl.BlockSpec((PAIR_ROWS * N_HEADS, 1), lambda b, pt: (0, 0)), pair_spec, new_spec, new_spec]
        + [page_spec(p) for p in range(n_pages)] * 2,
        out_specs=pair_spec,
        scratch_shapes=[pltpu.VMEM((PAGE_SIZE, D_ATT), F32), pltpu.VMEM((PAGE_SIZE, D_ATT), F32)],
    )
    out = pl.pallas_call(
        functools.partial(_attn_sample_kernel, n_pages),
        grid_spec=grid_spec,
        out_shape=jax.ShapeDtypeStruct((dec_b // 2, PAIR_ROWS, D_ATT), F32),
        compiler_params=_params(("arbitrary",)),
        name="attn_sample",
    )(page_table, bias_rows, q_s.reshape(dec_b // 2, PAIR_ROWS, D_ATT), k_new, v_new,
      *([flat_cache(cache_k)] * n_pages), *([flat_cache(cache_v)] * n_pages))
    return out.reshape(dec_b * DEC_SEQ, D_ATT)


def _pool_project(pooled, wp_ref, sp):
    outs = []
    for g in range(len(POOL_WINDOWS)):
        sl = slice(g * POOL_GROUP, (g + 1) * POOL_GROUP)
        outs.append(_dot(pooled[:, sl].astype(BF16), wp_ref[g]))
    return jnp.concatenate(outs, axis=1) * sp


def _rglru_inputs(xc, wa_ref, ba, wx_ref, bx, sp_lam):
    ra, ia = [], []
    for g in range(D_RNN // RNN_BLOCK):
        xg = xc[:, g * RNN_BLOCK:(g + 1) * RNN_BLOCK].astype(BF16)
        ra.append(_dot(xg, wa_ref[g]))
        ia.append(_dot(xg, wx_ref[g]))
    r = jax.nn.sigmoid(jnp.concatenate(ra, axis=1) + ba)
    i = jax.nn.sigmoid(jnp.concatenate(ia, axis=1) + bx)
    log_a = -RG_C * r * sp_lam
    a = jnp.exp(log_a)
    inp = jnp.sqrt(-jnp.tanh(log_a) * (a * a + 1.0)) * (i * xc)
    return a, inp


def _mix_prompt_kernel(nb, n_alias, *refs):
    rest_refs = refs[:nb]
    wp_ref, sp_ref, cw_ref, cb_ref, wa_ref, ba_ref, wx_ref, bx_ref, lam_ref = refs[nb:nb + 9]
    o_ref, h_ref, pool_ext, conv_ext, a_scr, b_scr, hs_scr, h_scr = refs[nb + 9 + n_alias:]
    tt = MIX_TILE
    step = pl.program_id(0)

    @pl.when(step == 0)
    def _():
        pool_ext[:, 0:HIST_ROWS, :] = jnp.zeros((nb, HIST_ROWS, D_POOL), F32)
        conv_ext[:, 0:CONV_HIST, :] = jnp.zeros((nb, CONV_HIST, D_RNN), F32)
        h_scr[...] = jnp.zeros_like(h_scr)

    pos = step * tt + lax.broadcasted_iota(jnp.int32, (tt, POOL_GROUP), 0)
    sp_lam = _softplus(-lam_ref[...])
    for b in range(nb):
        rest_ref = rest_refs[b]
        u = rest_ref[:, 0:D_POOL]
        pool_ext[b, HIST_ROWS:HIST_ROWS + tt, :] = u
        pooled = []
        for g, win in enumerate(POOL_WINDOWS):
            sl = slice(g * POOL_GROUP, (g + 1) * POOL_GROUP)
            s = u[:, sl]
            for j in range(1, win):
                s = s + pool_ext[b, HIST_ROWS - j:HIST_ROWS - j + tt, sl]
            cnt = jnp.minimum(win, pos + 1).astype(F32)
            pooled.append(s / cnt - u[:, sl])
        o_ref[b, :, 0:D_POOL] = _pool_project(jnp.concatenate(pooled, axis=1), wp_ref, sp_ref[...])
        pool_ext[b, 0:HIST_ROWS, :] = pool_ext[b, tt:tt + HIST_ROWS, :]

        ur = rest_ref[:, D_POOL:D_POOL + D_RNN]
        conv_ext[b, CONV_HIST:CONV_HIST + tt, :] = ur
        xc = cb_ref[...]
        for j in range(CONV_WIDTH):
            off = CONV_HIST - (CONV_WIDTH - 1) + j
            xc = xc + conv_ext[b, off:off + tt, :] * cw_ref[j:j + 1, :]
        conv_ext[b, 0:CONV_HIST, :] = conv_ext[b, tt:tt + CONV_HIST, :]
        a, inp = _rglru_inputs(xc, wa_ref, ba_ref[...], wx_ref, bx_ref[...], sp_lam)
        a_scr[b] = a
        b_scr[b] = inp

    def scan_row(t, hs):
        new = []
        for b in range(nb):
            h = a_scr[b, pl.ds(t, 1), :] * hs[b] + b_scr[b, pl.ds(t, 1), :]
            hs_scr[b, pl.ds(t, 1), :] = h
            new.append(h)
        return tuple(new)

    hs = lax.fori_loop(0, tt, scan_row, tuple(h_scr[b:b + 1, :] for b in range(nb)), unroll=8)
    for b in range(nb):
        h_scr[b:b + 1, :] = hs[b]
        o_ref[b, :, D_POOL:D_POOL + D_RNN] = hs_scr[b] * _gelu_tanh(rest_refs[b][:, D_POOL + D_RNN:D_REST])
    h_ref[0] = h_scr[0:nb, :]


def _mix_weight_specs(layer):
    n_blk = D_RNN // RNN_BLOCK
    return [_layer_block((len(POOL_WINDOWS), POOL_GROUP, POOL_GROUP), layer), _layer_block((1, D_POOL), layer),
            _layer_block((CONV_WIDTH, D_RNN), layer), _layer_block((1, D_RNN), layer),
            _layer_block((n_blk, RNN_BLOCK, RNN_BLOCK), layer), _layer_block((1, D_RNN), layer),
            _layer_block((n_blk, RNN_BLOCK, RNN_BLOCK), layer), _layer_block((1, D_RNN), layer),
            _layer_block((1, D_RNN), layer)]


def _mix_prompt(rest, mix_w, layer, batch, seq, h_buf):
    tt = MIX_TILE
    depth = mix_w[0].shape[0]
    steps = seq // tt
    alias_in = [] if h_buf is None else [h_buf]
    rest_specs = [pl.BlockSpec((tt, D_REST), functools.partial(lambda b, i: (b * steps + i, 0), b))
                  for b in range(batch)]
    return pl.pallas_call(
        functools.partial(_mix_prompt_kernel, batch, len(alias_in)),
        grid=(steps,),
        in_specs=rest_specs + _mix_weight_specs(layer) + [pl.BlockSpec(memory_space=pl.ANY)] * len(alias_in),
        out_specs=[pl.BlockSpec((batch, tt, D_POOL + D_RNN), lambda i: (0, i, 0)),
                   pl.BlockSpec((1, batch, D_RNN), lambda i: (layer, 0, 0))],
        out_shape=[jax.ShapeDtypeStruct((batch, seq, D_POOL + D_RNN), F32),
                   jax.ShapeDtypeStruct((depth, batch, D_RNN), F32)],
        scratch_shapes=[pltpu.VMEM((batch, HIST_ROWS + tt, D_POOL), F32),
                        pltpu.VMEM((batch, CONV_HIST + tt, D_RNN), F32),
                        pltpu.VMEM((batch, tt, D_RNN), F32),
                        pltpu.VMEM((batch, tt, D_RNN), F32),
                        pltpu.VMEM((batch, tt, D_RNN), F32),
                        pltpu.VMEM((SUBLANES, D_RNN), F32)],
        input_output_aliases={batch + len(mix_w) + j: 1 + j for j in range(len(alias_in))},
        compiler_params=_params(("arbitrary",)),
        name="mix_prompt",
    )(*([rest] * batch), *mix_w, *alias_in)


def _mix_sample_kernel(pos0, n_alias, rest_ref, pool_ref, conv_ref, h0_ref,
                       wp_ref, sp_ref, cw_ref, cb_ref, wa_ref, ba_ref, wx_ref, bx_ref, lam_ref, *refs):
    o_ref, h_ref = refs[n_alias:]
    steps = rest_ref.shape[0]
    sp_lam = _softplus(-lam_ref[...])
    pool_rows = [pool_ref[j] for j in range(POOL_BUF)] + [rest_ref[i, :, 0:D_POOL] for i in range(steps)]
    conv_rows = ([conv_ref[j] for j in range(CONV_WIDTH - 1)]
                 + [rest_ref[i, :, D_POOL:D_POOL + D_RNN] for i in range(steps)])
    h = h0_ref[...]
    for i in range(steps):
        u = pool_rows[POOL_BUF + i]
        pooled = []
        for g, win in enumerate(POOL_WINDOWS):
            sl = slice(g * POOL_GROUP, (g + 1) * POOL_GROUP)
            s = u[:, sl]
            for j in range(1, win):
                s = s + pool_rows[POOL_BUF + i - j][:, sl]
            pooled.append(s / float(min(win, pos0 + i + 1)) - u[:, sl])
        o_ref[i, :, 0:D_POOL] = _pool_project(jnp.concatenate(pooled, axis=1), wp_ref, sp_ref[...])

        xc = cb_ref[...]
        for j in range(CONV_WIDTH):
            xc = xc + conv_rows[i + j] * cw_ref[j:j + 1, :]
        a, inp = _rglru_inputs(xc, wa_ref, ba_ref[...], wx_ref, bx_ref[...], sp_lam)
        h = a * h + inp
        o_ref[i, :, D_POOL:D_POOL + D_RNN] = h * _gelu_tanh(rest_ref[i, :, D_POOL + D_RNN:D_REST])
    h_ref[0] = h


def _mix_sample(rest_t, pool_t, conv_t, state_h, mix_w, layer, pos0, h_buf):
    steps, dec_b, _ = rest_t.shape
    depth = state_h.shape[0]
    whole = lambda shape: pl.BlockSpec(shape, lambda i: (0,) * len(shape))
    alias_in = [] if h_buf is None else [h_buf]
    n_in = 4 + len(mix_w)
    return pl.pallas_call(
        functools.partial(_mix_sample_kernel, pos0, len(alias_in)),
        grid=(1,),
        in_specs=[whole(rest_t.shape), whole(pool_t.shape), whole(conv_t.shape),
                  pl.BlockSpec((None, dec_b, D_RNN), lambda i: (layer, 0, 0))]
        + _mix_weight_specs(layer) + [pl.BlockSpec(memory_space=pl.ANY)] * len(alias_in),
        out_specs=[whole((steps, dec_b, D_POOL + D_RNN)),
                   pl.BlockSpec((1, dec_b, D_RNN), lambda i: (layer, 0, 0))],
        out_shape=[jax.ShapeDtypeStruct((steps, dec_b, D_POOL + D_RNN), F32),
                   jax.ShapeDtypeStruct((depth, dec_b, D_RNN), F32)],
        input_output_aliases={n_in + j: 1 + j for j in range(len(alias_in))},
        compiler_params=_params(("arbitrary",)),
        name="mix_sample",
    )(rest_t, pool_t, conv_t, state_h, *mix_w, *alias_in)


def _out_proj_kernel(att_ref, pr_ref, x_ref, gg_ref, w_ref, gp_ref, gf_ref, o_ref, xn_ref):
    pr = pr_ref[...]
    mixed = jnp.concatenate(
        [_normalize(att_ref[...]), _normalize(pr[:, 0:D_POOL]), _normalize(pr[:, D_POOL:])],
        axis=1) * gg_ref[...]
    y = _dot(mixed.astype(BF16), w_ref[...])
    x = x_ref[...] + _rms(y, gp_ref[...])
    o_ref[...] = x
    xn_ref[...] = _rms(x, gf_ref[...]).astype(BF16)


def _out_proj(att, pr, x, g_grp, w_bf, g_post, g_pre_ffn, layer):
    n = x.shape[0]
    return pl.pallas_call(
        _out_proj_kernel,
        grid=(n // ROW_TILE,),
        in_specs=[_row_block(D_ATT), _row_block(D_POOL + D_RNN), _row_block(D_MODEL),
                  _layer_block((1, D_MODEL), layer), _whole_block((D_MODEL, D_MODEL)),
                  _layer_block((1, D_MODEL), layer), _layer_block((1, D_MODEL), layer)],
        out_specs=[_row_block(D_MODEL), _row_block(D_MODEL)],
        out_shape=[jax.ShapeDtypeStruct((n, D_MODEL), F32), jax.ShapeDtypeStruct((n, D_MODEL), BF16)],
        compiler_params=_params(("arbitrary",)),
        name="out_proj",
    )(att, pr, x, g_grp, w_bf, g_post, g_pre_ffn)


def _ffn_kernel(xn_ref, wu_ref, wd_ref, o_ref):
    @pl.when(pl.program_id(1) == 0)
    def _():
        o_ref[...] = jnp.zeros_like(o_ref)

    h = jnp.square(jnp.maximum(_dot(xn_ref[...], wu_ref[...]), 0.0))
    o_ref[...] += _dot(h.astype(BF16), wd_ref[...])


def _ffn(xn, wu_bf, wd_bf):
    n = xn.shape[0]
    row = pl.BlockSpec((ROW_TILE, D_MODEL), lambda i, f: (i, 0))
    return pl.pallas_call(
        _ffn_kernel,
        grid=(n // ROW_TILE, D_FF // FF_TILE),
        in_specs=[row, pl.BlockSpec((D_MODEL, FF_TILE), lambda i, f: (0, f)),
                  pl.BlockSpec((FF_TILE, D_MODEL), lambda i, f: (f, 0))],
        out_specs=row,
        out_shape=jax.ShapeDtypeStruct((n, D_MODEL), F32),
        compiler_params=_params(("arbitrary", "arbitrary")),
        name="ffn",
    )(xn, wu_bf, wd_bf)


def _ple_kernel(x_ref, f_ref, gf_ref, p_ref, wp_ref, g_ref, wg_ref, o_ref):
    x = x_ref[...] + _rms(f_ref[...], gf_ref[...])
    e = _rms(_dot(p_ref[...].astype(BF16), wp_ref[...]), g_ref[...])
    o_ref[...] = x + e * jax.nn.sigmoid(_dot(x.astype(BF16), wg_ref[...]))


def _ple(x, f, g_post_ffn, p, wp_bf, g, wg_bf, layer):
    n = x.shape[0]
    return pl.pallas_call(
        _ple_kernel,
        grid=(n // ROW_TILE,),
        in_specs=[_row_block(D_MODEL), _row_block(D_MODEL), _layer_block((1, D_MODEL), layer),
                  pl.BlockSpec((None, ROW_TILE, D_PLE), lambda i: (layer, i, 0)),
                  _layer_block((D_PLE, D_MODEL), layer), _layer_block((1, D_MODEL), layer),
                  _whole_block((D_MODEL, D_MODEL))],
        out_specs=_row_block(D_MODEL),
        out_shape=jax.ShapeDtypeStruct((n, D_MODEL), F32),
        compiler_params=_params(("arbitrary",)),
        name="ple",
    )(x, f, g_post_ffn, p, wp_bf, g, wg_bf)


def kernel(x_prompt, x_sample, p_prompt, p_sample, cache_k, cache_v, page_table, state_pool, state_conv, state_h, g_pre_mix, w_in, b_sb, w_pool, s_pool, conv_w, conv_b, w_a, b_a, w_x, b_x, lam, g_grp, w_out, g_post_mix, g_pre_ffn, w_up, w_down, g_post_ffn, w_ple, g_ple, w_ple_gate):
    depth = w_in.shape[0]
    batch, seq, _ = x_prompt.shape
    dec_b, dec_t, _ = x_sample.shape
    n_p, n_s = batch * seq, dec_b * dec_t
    past_len = page_table.shape[1] * PAGE_SIZE
    assert dec_t == DEC_SEQ and dec_b % 2 == 0 and n_p % ROW_TILE == 0 and n_s % ROW_TILE == 0
    assert seq % MIX_TILE == 0 and seq % ATT_TILE == 0 and past_len + dec_t > max(POOL_WINDOWS)

    bf = lambda a: a.astype(BF16)
    vec = lambda a: a.reshape(depth, 1, -1)
    w_in_bf = bf(w_in[0])
    w_ple_bf = bf(w_ple)
    mix_w = (bf(w_pool), vec(s_pool), conv_w, vec(conv_b), bf(w_a), vec(b_a), bf(w_x), vec(b_x), vec(lam))
    g_pre_mix, g_grp, g_post_mix, g_pre_ffn, g_post_ffn, g_ple = map(
        vec, (g_pre_mix, g_grp, g_post_mix, g_pre_ffn, g_post_ffn, g_ple))
    pp = p_prompt.reshape(depth, n_p, D_PLE)
    ps = p_sample.reshape(depth, n_s, D_PLE)
    pool_t = state_pool.transpose(0, 2, 1, 3)
    conv_t = state_conv.transpose(0, 2, 1, 3)

    xp = x_prompt.reshape(n_p, D_MODEL)
    xs = x_sample.reshape(n_s, D_MODEL)
    kv_p = kv_s = hp_buf = hs_buf = None
    pool_p, conv_p, pool_s, conv_s = [], [], [], []

    def dense_tail(att, pr, x, p, w_bf, l):
        w_out_bf, wu_bf, wd_bf, w_gate_bf = w_bf
        x, xn = _out_proj(att, pr, x, g_grp, w_out_bf, g_post_mix, g_pre_ffn, l)
        return _ple(x, _ffn(xn, wu_bf, wd_bf), g_post_ffn, p, w_ple_bf, g_ple, w_gate_bf, l)

    for l in range(depth):
        q, k, v, rest, *kv_p = _in_proj(xp, g_pre_mix, w_in_bf, l, kv_p)
        casts = [(w_out, l), (w_up, l), (w_down, l), (w_ple_gate, l)] + ([(w_in, l + 1)] if l + 1 < depth else [])
        att, *w_bf = _attn_prompt(q, k, v, b_sb[l], casts, batch, seq)
        w_in_next = w_bf.pop() if l + 1 < depth else None
        pr, hp_buf = _mix_prompt(rest, mix_w, l, batch, seq, hp_buf)
        xp = dense_tail(att, pr.reshape(n_p, -1), xp, pp, w_bf, l)
        tail = lambda rows, lo: jnp.stack(
            [rest[(b + 1) * seq - rows:(b + 1) * seq, lo:lo + D_POOL] for b in range(batch)])
        pool_p.append(tail(POOL_BUF, 0))
        conv_p.append(tail(CONV_WIDTH - 1, D_POOL))

        q, _, _, rest, *kv_s = _in_proj(xs, g_pre_mix, w_in_bf, l, kv_s)
        bias_rows = jnp.repeat(b_sb[l], PAIR_ROWS).reshape(PAIR_ROWS * N_HEADS, 1)
        att = _attn_sample(q.astype(F32), kv_s[0], kv_s[1], l, bias_rows, cache_k, cache_v, page_table)
        rest = rest.reshape(dec_b, dec_t, D_REST)
        pr, hs_buf = _mix_sample(rest.transpose(1, 0, 2), pool_t[l], conv_t[l], state_h,
                                 mix_w, l, past_len, hs_buf)
        xs = dense_tail(att, pr.transpose(1, 0, 2).reshape(n_s, -1), xs, ps, w_bf, l)
        pool_s.append(jnp.concatenate([state_pool[l], rest[:, :, 0:D_POOL]], axis=1)[:, -POOL_BUF:])
        conv_s.append(jnp.concatenate([state_conv[l], rest[:, :, D_POOL:D_POOL + D_RNN]],
                                      axis=1)[:, -(CONV_WIDTH - 1):])
        w_in_bf = w_in_next

    kp, vp = kv_p
    ks, vs = kv_s
    heads = lambda a, nb: a.reshape(depth, nb, -1, N_HEADS, HEAD_DIM)
    return (xp.reshape(batch, seq, D_MODEL), xs.reshape(dec_b, dec_t, D_MODEL),
            heads(kp, batch), heads(vp, batch), jnp.stack(pool_p), jnp.stack(conv_p), hp_buf,
            heads(ks, dec_b), heads(vs, dec_b), jnp.stack(pool_s), jnp.stack(conv_s), hs_buf)
```
